```python
import math
import jax, jax.numpy as jnp
from jax import lax
import numpy as np

D_MODEL = 2048
BATCH = 8
SEQ = 4096
DEPTH = 2

N_A_LAYERS = max(DEPTH // 2, 1)
N_B_LAYERS = DEPTH - N_A_LAYERS

SSM_EXPAND = 2
D_INNER = SSM_EXPAND * D_MODEL
SSM_HEAD_DIM = 64
SSM_HEADS = D_INNER // SSM_HEAD_DIM
SSM_GROUPS = 8
D_STATE = 128
D_CONV = 4
CHUNK = 128
GN = SSM_GROUPS * D_STATE
CONV_DIM = D_INNER + 2 * GN
D_IN_PROJ = D_INNER + CONV_DIM + SSM_HEADS

ATT_HEAD_DIM = 64
ATT_HEADS = D_MODEL // ATT_HEAD_DIM
KV_HEADS = 4
Q_PER_KV = ATT_HEADS // KV_HEADS
Q_WIDTH = ATT_HEADS * ATT_HEAD_DIM
KV_WIDTH = KV_HEADS * ATT_HEAD_DIM
WINDOW = 128
ROT_DIM = ATT_HEAD_DIM // 4
ROPE_THETA = 500000.0

ALPHA = (2.0 * DEPTH) ** 0.25
BETA = (8.0 * DEPTH) ** -0.25
EPS = 1e-5

kernel_name = "yoco_mamba2_swa_sink_hybrid"


def _layernorm(x, g, b):
    xf = x.astype(jnp.float32)
    mu = jnp.mean(xf, axis=-1, keepdims=True)
    var = jnp.mean(jnp.square(xf - mu), axis=-1, keepdims=True)
    y = (xf - mu) * lax.rsqrt(var + EPS) * g.astype(jnp.float32) + b.astype(jnp.float32)
    return y.astype(x.dtype)


def _segsum(x):
    t = x.shape[-1]
    xe = jnp.broadcast_to(x[..., :, None], x.shape + (t,))
    xe = jnp.where(jnp.tril(jnp.ones((t, t), dtype=bool), -1), xe, 0.0)
    s = jnp.cumsum(xe, axis=-2)
    return jnp.where(jnp.tril(jnp.ones((t, t), dtype=bool)), s, -jnp.inf)


def _ssd_chunked(xs, dt, a, bm, cm):
    bsz, seqlen, nh, hp = xs.shape
    ng = bm.shape[2]
    rg = nh // ng
    nc = seqlen // CHUNK
    x_c = (xs * dt[..., None]).reshape(bsz, nc, CHUNK, ng, rg, hp)
    b_c = bm.reshape(bsz, nc, CHUNK, ng, D_STATE)
    c_c = cm.reshape(bsz, nc, CHUNK, ng, D_STATE)
    da = (dt * a).reshape(bsz, nc, CHUNK, ng, rg).transpose(0, 3, 4, 1, 2)
    da_cum = jnp.cumsum(da, axis=-1)
    decay_in = jnp.exp(_segsum(da))
    cb = jnp.einsum("bcqgn,bcsgn->bgcqs", c_c, b_c)
    y_diag = jnp.einsum("bgcqs,bgrcqs,bcsgrp->bcqgrp", cb, decay_in, x_c)
    decay_st = jnp.exp(da_cum[..., -1:] - da_cum)
    states = jnp.einsum("bcsgn,bgrcs,bcsgrp->bcgrpn", b_c, decay_st, x_c)
    chunk_tot = jnp.pad(da_cum[..., -1], ((0, 0), (0, 0), (0, 0), (1, 0)))
    decay_ch = jnp.exp(_segsum(chunk_tot))
    states0 = jnp.concatenate([jnp.zeros_like(states[:, :1]), states], axis=1)
    start_states = jnp.einsum("bgrzc,bcgrpn->bzgrpn", decay_ch, states0)[:, :-1]
    y_off = jnp.einsum("bcqgn,bcgrpn,bgrcq->bcqgrp", c_c, start_states, jnp.exp(da_cum))
    return (y_diag + y_off).reshape(bsz, seqlen, nh, hp)


def _mamba2_mixer(x, w_in, conv_w, conv_b, dt_bias, a_log, d_skip, norm_w, w_out):
    bsz, seqlen, _ = x.shape
    zxbcdt = x @ w_in
    z = zxbcdt[..., :D_INNER]
    xbc = zxbcdt[..., D_INNER:D_INNER + CONV_DIM]
    dt_raw = zxbcdt[..., D_INNER + CONV_DIM:]
    xbc = lax.conv_general_dilated(
        xbc, conv_w[:, None, :].astype(xbc.dtype), window_strides=(1,),
        padding=[(D_CONV - 1, 0)], dimension_numbers=("NWC", "WIO", "NWC"),
        feature_group_count=CONV_DIM) + conv_b
    xbc = jax.nn.silu(xbc)
    xs = xbc[..., :D_INNER].reshape(bsz, seqlen, SSM_HEADS, SSM_HEAD_DIM).astype(jnp.float32)
    bm = xbc[..., D_INNER:D_INNER + GN].reshape(bsz, seqlen, SSM_GROUPS, D_STATE).astype(jnp.float32)
    cm = xbc[..., D_INNER + GN:].reshape(bsz, seqlen, SSM_GROUPS, D_STATE).astype(jnp.float32)
    dt = jax.nn.softplus(dt_raw.astype(jnp.float32) + dt_bias.astype(jnp.float32))
    a = -jnp.exp(a_log.astype(jnp.float32))
    y = _ssd_chunked(xs, dt, a, bm, cm) + xs * d_skip.astype(jnp.float32)[:, None]
    y = y.reshape(bsz, seqlen, D_INNER) * jax.nn.silu(z.astype(jnp.float32))
    yg = y.reshape(bsz, seqlen, SSM_GROUPS, D_INNER // SSM_GROUPS)
    yg = yg * lax.rsqrt(jnp.mean(jnp.square(yg), axis=-1, keepdims=True) + EPS)
    y = yg.reshape(bsz, seqlen, D_INNER) * norm_w.astype(jnp.float32)
    return y.astype(x.dtype) @ w_out


def _rope_tables(positions):
    inv_freq = ROPE_THETA ** (-jnp.arange(0, ROT_DIM, 2, dtype=jnp.float32) / ROT_DIM)
    ang = positions.astype(jnp.float32)[..., None] * inv_freq
    return jnp.cos(ang), jnp.sin(ang)


def _rope_partial(x, cos, sin):
    half = ROT_DIM // 2
    xf = x.astype(jnp.float32)
    x1, x2, rest = xf[..., :half], xf[..., half:ROT_DIM], xf[..., ROT_DIM:]
    out = jnp.concatenate([x1 * cos - x2 * sin, x2 * cos + x1 * sin, rest], axis=-1)
    return out.astype(x.dtype)


def _shared_kv(x, kv_w, kv_b, cos, sin):
    bsz, seqlen, _ = x.shape
    kv = x @ kv_w + kv_b
    k = kv[..., :KV_WIDTH].reshape(bsz, seqlen, KV_HEADS, ATT_HEAD_DIM)
    v = kv[..., KV_WIDTH:].reshape(bsz, seqlen, KV_HEADS, ATT_HEAD_DIM)
    k = _rope_partial(k, cos[:, :, None, :], sin[:, :, None, :])
    return k, v


def _swa_sink_attention(q, k, v, sinks):
    bsz, seqlen, hk, rq, hd = q.shape
    nb = seqlen // WINDOW
    qb = q.reshape(bsz, nb, WINDOW, hk, rq, hd)
    kb = k.reshape(bsz, nb, WINDOW, hk, hd)
    vb = v.reshape(bsz, nb, WINDOW, hk, hd)
    pad = ((0, 0), (1, 0), (0, 0), (0, 0), (0, 0))
    kk = jnp.concatenate([jnp.pad(kb, pad)[:, :-1], kb], axis=2)
    vv = jnp.concatenate([jnp.pad(vb, pad)[:, :-1], vb], axis=2)
    s = jnp.einsum("bcqhrd,bckhd->bchrqk", qb, kk).astype(jnp.float32) * (hd ** -0.5)
    qi = jnp.arange(WINDOW)[:, None] + WINDOW
    ki = jnp.arange(2 * WINDOW)[None, :]
    diff = qi - ki
    kpos = jnp.arange(nb)[:, None, None] * WINDOW + ki[None] - WINDOW
    valid = (diff >= 0)[None] & (diff < WINDOW)[None] & (kpos >= 0)
    s = jnp.where(valid[None, :, None, None], s, -jnp.inf)
    sk = sinks.astype(jnp.float32)[None, None, :, :, None, None]
    m = jnp.maximum(jnp.max(s, axis=-1, keepdims=True), sk)
    p = jnp.exp(s - m)
    p = p / (jnp.sum(p, axis=-1, keepdims=True) + jnp.exp(sk - m))
    o = jnp.einsum("bchrqk,bckhd->bcqhrd", p.astype(vv.dtype), vv)
    return o.reshape(bsz, seqlen, hk * rq * hd)


def _swa_mixer(x, k, v, cos, sin, w_in, q_bias, sinks, w_out):
    bsz, seqlen, _ = x.shape
    proj = x @ w_in
    q = (proj[..., :Q_WIDTH] + q_bias).reshape(bsz, seqlen, KV_HEADS, Q_PER_KV, ATT_HEAD_DIM)
    gate = proj[..., Q_WIDTH:]
    q = _rope_partial(q, cos[:, :, None, None, :], sin[:, :, None, None, :])
    o = _swa_sink_attention(q, k, v, sinks.reshape(KV_HEADS, Q_PER_KV))
    o = o * jax.nn.silu(gate)
    return o @ w_out


def setup_inputs(seed: int = 0) -> dict:
    key = jax.random.key(seed)
    ks = jax.random.split(key, 20)
    f32 = jnp.float32
    x = jax.random.normal(ks[0], (BATCH, SEQ, D_MODEL), f32)
    offs = jax.random.randint(ks[1], (BATCH, 1), 0, 1024, dtype=jnp.int32)
    positions = (offs + jnp.arange(SEQ, dtype=jnp.int32)[None, :]).astype(jnp.int32)
    ln_g = 1.0 + 0.02 * jax.random.normal(ks[2], (DEPTH, D_MODEL), f32)
    ln_b = 0.02 * jax.random.normal(ks[3], (DEPTH, D_MODEL), f32)
    a_w_in = jax.random.normal(ks[4], (N_A_LAYERS, D_MODEL, D_IN_PROJ), f32) * D_MODEL ** -0.5
    a_conv_w = jax.random.normal(ks[5], (N_A_LAYERS, D_CONV, CONV_DIM), f32) * D_CONV ** -0.5
    a_conv_b = 0.02 * jax.random.normal(ks[6], (N_A_LAYERS, CONV_DIM), f32)
    dt0 = jnp.exp(jax.random.uniform(ks[7], (N_A_LAYERS, SSM_HEADS), f32,
                                     math.log(1e-3), math.log(1e-1)))
    a_dt_bias = dt0 + jnp.log(-jnp.expm1(-dt0))
    a_log = jnp.log(jax.random.uniform(ks[8], (N_A_LAYERS, SSM_HEADS), f32, 1.0, 16.0))
    a_d = 1.0 + 0.1 * jax.random.normal(ks[9], (N_A_LAYERS, SSM_HEADS), f32)
    a_norm_w = 1.0 + 0.02 * jax.random.normal(ks[10], (N_A_LAYERS, D_INNER), f32)
    a_w_out = jax.random.normal(ks[11], (N_A_LAYERS, D_INNER, D_MODEL), f32) * (D_INNER ** -0.5 * BETA)
    kv_scale = jnp.concatenate([jnp.ones((KV_WIDTH,), f32), jnp.full((KV_WIDTH,), BETA, f32)])
    kv_w = jax.random.normal(ks[12], (D_MODEL, 2 * KV_WIDTH), f32) * D_MODEL ** -0.5 * kv_scale
    kv_b = 0.02 * jax.random.normal(ks[13], (2 * KV_WIDTH,), f32)
    b_w_in = jax.random.normal(ks[14], (N_B_LAYERS, D_MODEL, 2 * Q_WIDTH), f32) * D_MODEL ** -0.5
    b_q_bias = 0.02 * jax.random.normal(ks[15], (N_B_LAYERS, Q_WIDTH), f32)
    b_sinks = 0.5 * jax.random.normal(ks[16], (N_B_LAYERS, ATT_HEADS), f32)
    b_w_out = jax.random.normal(ks[17], (N_B_LAYERS, Q_WIDTH, D_MODEL), f32) * (Q_WIDTH ** -0.5 * BETA)
    return {"x": x, "positions": positions, "ln_g": ln_g, "ln_b": ln_b,
            "a_w_in": a_w_in, "a_conv_w": a_conv_w, "a_conv_b": a_conv_b,
            "a_dt_bias": a_dt_bias, "a_log": a_log, "a_d": a_d, "a_norm_w": a_norm_w,
            "a_w_out": a_w_out, "kv_w": kv_w, "kv_b": kv_b, "b_w_in": b_w_in,
            "b_q_bias": b_q_bias, "b_sinks": b_sinks, "b_w_out": b_w_out}


def reference(x, positions, ln_g, ln_b, a_w_in, a_conv_w, a_conv_b, a_dt_bias, a_log, a_d,
              a_norm_w, a_w_out, kv_w, kv_b, b_w_in, b_q_bias, b_sinks, b_w_out):
    cos, sin = _rope_tables(positions)
    k_sh = None
    v_sh = None
    for layer in range(DEPTH):
        if layer < N_A_LAYERS:
            i = layer
            h = _mamba2_mixer(x, a_w_in[i], a_conv_w[i], a_conv_b[i], a_dt_bias[i], a_log[i],
                              a_d[i], a_norm_w[i], a_w_out[i])
        else:
            if layer == N_A_LAYERS:
                k_sh, v_sh = _shared_kv(x, kv_w, kv_b, cos, sin)
            j = layer - N_A_LAYERS
            h = _swa_mixer(x, k_sh, v_sh, cos, sin, b_w_in[j], b_q_bias[j], b_sinks[j], b_w_out[j])
        x = _layernorm(ALPHA * x + h, ln_g[layer], ln_b[layer])
    return x
```

```python
import functools
import math

import numpy as np
import jax
import jax.numpy as jnp
from jax import lax
from jax.experimental import pallas as pl
from jax.experimental.pallas import tpu as pltpu

D_MODEL = 2048
BATCH = 8
SEQ = 4096
TOKENS = BATCH * SEQ
DEPTH = 2

D_INNER = 4096
SSM_HEAD_DIM = 64
SSM_HEADS = 64
SSM_GROUPS = 8
HEADS_PER_GROUP = SSM_HEADS // SSM_GROUPS
GROUP_WIDTH = D_INNER // SSM_GROUPS
D_STATE = 128
D_CONV = 4
CHUNK = 128
GN = SSM_GROUPS * D_STATE
CONV_DIM = D_INNER + 2 * GN
ZX_WIDTH = D_INNER + CONV_DIM

ATT_HEAD_DIM = 64
ATT_HEADS = 32
KV_HEADS = 4
Q_PER_KV = 8
Q_WIDTH = 2048
KV_WIDTH = 256
WINDOW = 128
ROT_DIM = 16
ROPE_THETA = 500000.0
KV_PACK = 4 * ATT_HEAD_DIM
QGK_WIDTH = 2 * Q_WIDTH + KV_HEADS * KV_PACK

ALPHA = (2.0 * DEPTH) ** 0.25
EPS = 1e-5

LANES = 128
SUBLANES = 8
BF16_SUBLANES = 16
VMEM_LIMIT_BYTES = 56 * 1024 * 1024

F32 = jnp.float32
BF16 = jnp.bfloat16


def _silu(v):
    return v * (1.0 / (1.0 + jnp.exp(-v)))


def _split_bf16(v, terms):
    parts = []
    rem = v
    for _ in range(terms):
        p = rem.astype(BF16)
        parts.append(p)
        rem = rem - p.astype(F32)
    return parts


PROJ_BM = 1024
PROJ_BN = 1024


def _in_proj_kernel(x_ref, w_ref, wdt_ref, zx_ref, dt_ref, xb_ref):
    @pl.when(pl.program_id(1) == 0)
    def _():
        xb_ref[...] = x_ref[...].astype(BF16)
        dt_ref[...] = jnp.dot(xb_ref[...], wdt_ref[...], preferred_element_type=F32)

    zx_ref[...] = jnp.dot(xb_ref[...], w_ref[...], preferred_element_type=F32).astype(BF16)


def _in_proj(x2, w_main, w_dt):
    grid = (TOKENS // PROJ_BM, ZX_WIDTH // PROJ_BN)
    return pl.pallas_call(
        _in_proj_kernel,
        grid=grid,
        in_specs=[
            pl.BlockSpec((PROJ_BM, D_MODEL), lambda i, j: (i, 0)),
            pl.BlockSpec((D_MODEL, PROJ_BN), lambda i, j: (0, j)),
            pl.BlockSpec((D_MODEL, LANES), lambda i, j: (0, 0)),
        ],
        out_specs=[
            pl.BlockSpec((PROJ_BM, PROJ_BN), lambda i, j: (i, j)),
            pl.BlockSpec((PROJ_BM, LANES), lambda i, j: (i, 0)),
        ],
        out_shape=[
            jax.ShapeDtypeStruct((TOKENS, ZX_WIDTH), BF16),
            jax.ShapeDtypeStruct((TOKENS, LANES), F32),
        ],
        scratch_shapes=[pltpu.VMEM((PROJ_BM, D_MODEL), BF16)],
        compiler_params=pltpu.CompilerParams(
            dimension_semantics=("arbitrary", "arbitrary"),
            vmem_limit_bytes=VMEM_LIMIT_BYTES),
        name="in_proj",
    )(x2, w_main, w_dt)


XS_HALF = D_INNER // 2
EXPAND_ROWS = 3 * CHUNK + BF16_SUBLANES


def _ssd_kernel(z_ref, xa_ref, xb_ref, bc_ref, dt_ref, cw_ref, cb_ref, dtb_ref, alog_ref,
                dsk_ref, nw_ref, e2_ref, y_ref, state_ref, halo_ref):
    @pl.when(pl.program_id(1) == 0)
    def _():
        state_ref[...] = jnp.zeros_like(state_ref)
        halo_ref[...] = jnp.zeros_like(halo_ref)

    def conv_silu(src_ref, lo, glo, width):
        cur = src_ref[:, lo:lo + width].astype(F32)
        prev = halo_ref[:, glo:glo + width]
        acc = cur * cw_ref[D_CONV - 1:D_CONV, glo:glo + width] + cb_ref[:, glo:glo + width]
        row = lax.broadcasted_iota(jnp.int32, (SUBLANES, width), 0)
        for j in range(1, D_CONV):
            rolled = pltpu.roll(cur, j, axis=0)
            fix = pltpu.roll(prev, j, axis=0)
            top = jnp.where(row < j, fix, rolled[0:SUBLANES])
            shifted = jnp.concatenate([top, rolled[SUBLANES:]], axis=0)
            acc = acc + shifted * cw_ref[D_CONV - 1 - j:D_CONV - j, glo:glo + width]
        return _silu(acc)

    v = dt_ref[...] + dtb_ref[...]
    dt = jnp.maximum(v, 0.0) + jnp.log1p(jnp.exp(-jnp.abs(v)))
    da = dt * (-jnp.exp(alog_ref[...]))
    rq = lax.broadcasted_iota(jnp.int32, (CHUNK, CHUNK), 0)
    cs = lax.broadcasted_iota(jnp.int32, (CHUNK, CHUNK), 1)
    tri = rq >= cs
    cum3 = jnp.dot(jnp.where(tri, 1.0, 0.0).astype(BF16),
                   jnp.concatenate(_split_bf16(da, 3), axis=1), preferred_element_type=F32)
    cum = cum3[:, 0:LANES] + cum3[:, LANES:2 * LANES] + cum3[:, 2 * LANES:3 * LANES]
    cum_t = cum.T
    last = cum[CHUNK - 1:CHUNK, :]
    stack = jnp.concatenate(
        [dt, jnp.exp(cum), jnp.exp(last - cum),
         jnp.broadcast_to(jnp.exp(last), (BF16_SUBLANES, LANES))], axis=0)
    expand_lhs = jnp.concatenate(_split_bf16(stack, 2), axis=1)

    lane = lax.broadcasted_iota(jnp.int32, (CHUNK, LANES), 1)
    low_half = lane < SSM_HEAD_DIM

    for g in range(SSM_GROUPS):
        gs = slice(g * GROUP_WIDTH, (g + 1) * GROUP_WIDTH)
        ex = jnp.dot(expand_lhs, e2_ref[:, gs], preferred_element_type=F32)
        dt_x = ex[0:CHUNK]
        ecum_x = ex[CHUNK:2 * CHUNK]
        dstate_x = ex[2 * CHUNK:3 * CHUNK]
        elast_x = ex[3 * CHUNK:3 * CHUNK + 1]

        if g < SSM_GROUPS // 2:
            xs = conv_silu(xa_ref, g * GROUP_WIDTH, g * GROUP_WIDTH, GROUP_WIDTH)
        else:
            xs = conv_silu(xb_ref, g * GROUP_WIDTH - XS_HALF, g * GROUP_WIDTH, GROUP_WIDTH)
        bm = conv_silu(bc_ref, g * D_STATE, D_INNER + g * D_STATE, D_STATE)
        cm = conv_silu(bc_ref, GN + g * D_STATE, D_INNER + GN + g * D_STATE, D_STATE)

        xdt = xs * dt_x
        bb = bm.astype(BF16)
        cb = cm.astype(BF16)
        cbt = lax.dot_general(cb, bb, (((1,), (1,)), ((), ())), preferred_element_type=F32)

        s_prev = state_ref[:, gs]
        y_off = jnp.dot(cb, s_prev.astype(BF16), preferred_element_type=F32) * ecum_x
        state_ref[:, gs] = s_prev * elast_x + jnp.dot(
            bm.T.astype(BF16), (xdt * dstate_x).astype(BF16), preferred_element_type=F32)

        pairs = []
        for p in range(HEADS_PER_GROUP // 2):
            xp = xdt[:, p * LANES:(p + 1) * LANES]
            acc = None
            for half in range(2):
                h = g * HEADS_PER_GROUP + 2 * p + half
                decay = jnp.where(tri, jnp.exp(cum[:, h:h + 1] - cum_t[h:h + 1, :]), 0.0)
                m = (cbt * decay).astype(BF16)
                xh = jnp.where(low_half if half == 0 else ~low_half, xp, 0.0).astype(BF16)
                part = jnp.dot(m, xh, preferred_element_type=F32)
                acc = part if acc is None else acc + part
            pairs.append(acc)
        y = jnp.concatenate(pairs, axis=1) + y_off + xs * dsk_ref[:, gs]
        y = y * _silu(z_ref[:, gs].astype(F32))
        ms = jnp.mean(y * y, axis=-1, keepdims=True)
        y_ref[:, gs] = (y * lax.rsqrt(ms + EPS) * nw_ref[:, gs]).astype(BF16)

    for k, src in enumerate((xa_ref, xb_ref, bc_ref)):
        tail = src[CHUNK - BF16_SUBLANES:CHUNK, :].astype(F32)
        halo_ref[:, k * XS_HALF:(k + 1) * XS_HALF] = tail[BF16_SUBLANES - SUBLANES:]


def _ssd(zx, dt_raw, conv_w, conv_b, dt_bias, a_log, d_skip_x, norm_w, e2):
    nc = SEQ // CHUNK
    row = lambda b, c: b * nc + c
    const = lambda b, c: (0, 0)
    zx_cols = ZX_WIDTH // XS_HALF
    assert zx_cols == 5
    return pl.pallas_call(
        _ssd_kernel,
        grid=(BATCH, nc),
        in_specs=[
            pl.BlockSpec((CHUNK, D_INNER), lambda b, c: (row(b, c), 0)),
            pl.BlockSpec((CHUNK, XS_HALF), lambda b, c: (row(b, c), 2)),
            pl.BlockSpec((CHUNK, XS_HALF), lambda b, c: (row(b, c), 3)),
            pl.BlockSpec((CHUNK, XS_HALF), lambda b, c: (row(b, c), 4)),
            pl.BlockSpec((CHUNK, LANES), lambda b, c: (row(b, c), 0)),
            pl.BlockSpec((D_CONV, CONV_DIM), const),
            pl.BlockSpec((1, CONV_DIM), const),
            pl.BlockSpec((1, LANES), const),
            pl.BlockSpec((1, LANES), const),
            pl.BlockSpec((1, D_INNER), const),
            pl.BlockSpec((1, D_INNER), const),
            pl.BlockSpec((2 * LANES, D_INNER), const),
        ],
        out_specs=pl.BlockSpec((CHUNK, D_INNER), lambda b, c: (row(b, c), 0)),
        out_shape=jax.ShapeDtypeStruct((TOKENS, D_INNER), BF16),
        scratch_shapes=[
            pltpu.VMEM((D_STATE, D_INNER), F32),
            pltpu.VMEM((SUBLANES, CONV_DIM), F32),
        ],
        compiler_params=pltpu.CompilerParams(
            dimension_semantics=("arbitrary", "arbitrary"),
            vmem_limit_bytes=VMEM_LIMIT_BYTES),
        name="ssd",
    )(zx, zx, zx, zx, dt_raw, conv_w, conv_b, dt_bias, a_log, d_skip_x, norm_w, e2)


def _out_ln_kernel(y_ref, w_ref, x_ref, g_ref, b_ref, o_ref):
    h = jnp.dot(y_ref[...], w_ref[...], preferred_element_type=F32)
    v = ALPHA * x_ref[...] + h
    mu = jnp.mean(v, axis=-1, keepdims=True)
    d = v - mu
    var = jnp.mean(d * d, axis=-1, keepdims=True)
    o_ref[...] = d * lax.rsqrt(var + EPS) * g_ref[...] + b_ref[...]


def _out_ln(y, w, x2, g, b, bm, name):
    k = y.shape[1]
    return pl.pallas_call(
        _out_ln_kernel,
        grid=(TOKENS // bm,),
        in_specs=[
            pl.BlockSpec((bm, k), lambda i: (i, 0)),
            pl.BlockSpec((k, D_MODEL), lambda i: (0, 0), pipeline_mode=pl.Buffered(1)),
            pl.BlockSpec((bm, D_MODEL), lambda i: (i, 0)),
            pl.BlockSpec((1, D_MODEL), lambda i: (0, 0)),
            pl.BlockSpec((1, D_MODEL), lambda i: (0, 0)),
        ],
        out_specs=pl.BlockSpec((bm, D_MODEL), lambda i: (i, 0)),
        out_shape=jax.ShapeDtypeStruct((TOKENS, D_MODEL), F32),
        compiler_params=pltpu.CompilerParams(
            dimension_semantics=("arbitrary",),
            vmem_limit_bytes=VMEM_LIMIT_BYTES),
        name=name,
    )(y, w, x2, g, b)


ROPE_SHIFT = ROT_DIM // 2
QGK_BM = 512


def _qgk_kernel(x_ref, pos_ref, invf_ref, w_ref, bias_ref, mask_ref, o_ref, xb_ref, cos_ref, sin_ref):
    j = pl.program_id(1)

    @pl.when(j == 0)
    def _():
        xb_ref[...] = x_ref[...].astype(BF16)
        ang = pos_ref[...].astype(F32) * invf_ref[...]
        cos_ref[...] = jnp.cos(ang)
        sin_ref[...] = jnp.sin(ang)

    acc = jnp.dot(xb_ref[...], w_ref[...], preferred_element_type=F32) + bias_ref[...]
    q_tiles = Q_WIDTH // PROJ_BN
    has_rope = (j < q_tiles) | (j >= 2 * q_tiles)

    @pl.when(has_rope)
    def _():
        reps = PROJ_BN // LANES
        cos_f = jnp.tile(cos_ref[...], (1, reps))
        sin_f = jnp.tile(sin_ref[...], (1, reps))
        m = mask_ref[...]
        upper = pltpu.roll(acc, PROJ_BN - ROPE_SHIFT, axis=1)
        lower = pltpu.roll(acc, ROPE_SHIFT, axis=1)
        out = (acc * (1.0 + m[0:1] * (cos_f - 1.0))
               + upper * (m[1:2] * sin_f) + lower * (m[2:3] * sin_f))
        o_ref[...] = out.astype(BF16)

    @pl.when(jnp.logical_not(has_rope))
    def _():
        o_ref[...] = acc.astype(BF16)


def _qgk_proj(x1, pos_b, invf, w, bias, masks):
    grid = (TOKENS // QGK_BM, QGK_WIDTH // PROJ_BN)
    return pl.pallas_call(
        _qgk_kernel,
        grid=grid,
        in_specs=[
            pl.BlockSpec((QGK_BM, D_MODEL), lambda i, j: (i, 0)),
            pl.BlockSpec((QGK_BM, LANES), lambda i, j: (i, 0)),
            pl.BlockSpec((1, LANES), lambda i, j: (0, 0)),
            pl.BlockSpec((D_MODEL, PROJ_BN), lambda i, j: (0, j)),
            pl.BlockSpec((1, PROJ_BN), lambda i, j: (0, j)),
            pl.BlockSpec((SUBLANES, PROJ_BN), lambda i, j: (0, j)),
        ],
        out_specs=pl.BlockSpec((QGK_BM, PROJ_BN), lambda i, j: (i, j)),
        out_shape=jax.ShapeDtypeStruct((TOKENS, QGK_WIDTH), BF16),
        scratch_shapes=[
            pltpu.VMEM((QGK_BM, D_MODEL), BF16),
            pltpu.VMEM((QGK_BM, LANES), F32),
            pltpu.VMEM((QGK_BM, LANES), F32),
        ],
        compiler_params=pltpu.CompilerParams(
            dimension_semantics=("arbitrary", "arbitrary"),
            vmem_limit_bytes=VMEM_LIMIT_BYTES),
        name="qgk_proj",
    )(x1, pos_b, invf, w, bias, masks)


PAIRS_PER_KV = Q_PER_KV // 2
STACK_ROWS = PAIRS_PER_KV * WINDOW
KV_GROUP_Q = Q_PER_KV * ATT_HEAD_DIM


def _attn_kernel(sink_ref, q_ref, gate_ref, kvp_ref, kvc_ref, o_ref):
    c = pl.program_id(1)
    row = lax.broadcasted_iota(jnp.int32, (STACK_ROWS, 2 * WINDOW), 0)
    col = lax.broadcasted_iota(jnp.int32, (STACK_ROWS, 2 * WINDOW), 1)
    qi = row & (WINDOW - 1)
    valid = (col > qi) & (col <= qi + WINDOW) & ((col >= WINDOW) | (c > 0))
    lane = lax.broadcasted_iota(jnp.int32, (STACK_ROWS, LANES), 1)
    low_half = lane < ATT_HEAD_DIM
    pair_of_row = lax.broadcasted_iota(jnp.int32, (STACK_ROWS, 1), 0) // WINDOW
    scale = ATT_HEAD_DIM ** -0.5

    for hk in range(KV_HEADS):
        ks = slice(hk * KV_PACK, (hk + 1) * KV_PACK)
        kv = jnp.concatenate([kvp_ref[:, ks], kvc_ref[:, ks]], axis=0)
        kv_kv = kv[:, 0:LANES]
        kv_vk = kv[:, LANES:2 * LANES]
        qs = jnp.concatenate(
            [q_ref[:, hk * KV_GROUP_Q + p * LANES:hk * KV_GROUP_Q + (p + 1) * LANES]
             for p in range(PAIRS_PER_KV)], axis=0)
        outs = []
        for half in range(2):
            qm = jnp.where(low_half if half == 0 else ~low_half, qs, jnp.zeros_like(qs))
            keys = kv_kv if half == 0 else kv_vk
            s = lax.dot_general(qm, keys, (((1,), (1,)), ((), ())), preferred_element_type=F32) * scale
            s = jnp.where(valid, s, -jnp.inf)
            sk = jnp.zeros((STACK_ROWS, 1), F32)
            for p in range(PAIRS_PER_KV):
                sk = jnp.where(pair_of_row == p, sink_ref[hk * Q_PER_KV + 2 * p + half], sk)
            m = jnp.maximum(jnp.max(s, axis=-1, keepdims=True), sk)
            pe = jnp.exp(s - m)
            den = jnp.sum(pe, axis=-1, keepdims=True) + jnp.exp(sk - m)
            vals = kv_vk if half == 0 else kv_kv
            outs.append(jnp.dot(pe.astype(BF16), vals, preferred_element_type=F32) / den)
        o_pairs = jnp.where(low_half, outs[0], outs[1])
        for p in range(PAIRS_PER_KV):
            cols = slice(hk * KV_GROUP_Q + p * LANES, hk * KV_GROUP_Q + (p + 1) * LANES)
            o_ref[:, cols] = (o_pairs[p * WINDOW:(p + 1) * WINDOW]
                              * _silu(gate_ref[:, cols].astype(F32))).astype(BF16)


def _attention(qgk, sinks):
    nb = SEQ // WINDOW
    row = lambda b, c: b * nb + c
    kv_col = (2 * Q_WIDTH) // (KV_HEADS * KV_PACK)
    return pl.pallas_call(
        _attn_kernel,
        grid=(BATCH, nb),
        in_specs=[
            pl.BlockSpec(memory_space=pltpu.SMEM),
            pl.BlockSpec((WINDOW, Q_WIDTH), lambda b, c: (row(b, c), 0)),
            pl.BlockSpec((WINDOW, Q_WIDTH), lambda b, c: (row(b, c), 1)),
            pl.BlockSpec((WINDOW, KV_HEADS * KV_PACK), lambda b, c: (row(b, jnp.maximum(c - 1, 0)), kv_col)),
            pl.BlockSpec((WINDOW, KV_HEADS * KV_PACK), lambda b, c: (row(b, c), kv_col)),
        ],
        out_specs=pl.BlockSpec((WINDOW, Q_WIDTH), lambda b, c: (row(b, c), 0)),
        out_shape=jax.ShapeDtypeStruct((TOKENS, Q_WIDTH), BF16),
        compiler_params=pltpu.CompilerParams(
            dimension_semantics=("arbitrary", "arbitrary"),
            vmem_limit_bytes=VMEM_LIMIT_BYTES),
        name="swa_attention",
    )(sinks, qgk, qgk, qgk, qgk)


def _expand_matrix():
    e = np.zeros((2 * LANES, D_INNER), np.float32)
    for h in range(SSM_HEADS):
        e[h, h * SSM_HEAD_DIM:(h + 1) * SSM_HEAD_DIM] = 1.0
        e[LANES + h, h * SSM_HEAD_DIM:(h + 1) * SSM_HEAD_DIM] = 1.0
    return jnp.asarray(e, BF16)


def _rope_masks():
    m = np.zeros((SUBLANES, QGK_WIDTH), np.float32)
    d = np.arange(ATT_HEAD_DIM)
    sect = np.stack([1.0 * (d < ROT_DIM), -1.0 * (d < ROPE_SHIFT),
                     1.0 * ((d >= ROPE_SHIFT) & (d < ROT_DIM))]).astype(np.float32)
    for s in range(Q_WIDTH // ATT_HEAD_DIM):
        m[0:3, s * ATT_HEAD_DIM:(s + 1) * ATT_HEAD_DIM] = sect
    base = 2 * Q_WIDTH
    for hk in range(KV_HEADS):
        for s in (0, 3):
            lo = base + hk * KV_PACK + s * ATT_HEAD_DIM
            m[0:3, lo:lo + ATT_HEAD_DIM] = sect
    return jnp.asarray(m)


def _pack_kv(a):
    k = a[..., :KV_WIDTH].reshape(a.shape[:-1] + (KV_HEADS, ATT_HEAD_DIM))
    v = a[..., KV_WIDTH:].reshape(a.shape[:-1] + (KV_HEADS, ATT_HEAD_DIM))
    return jnp.concatenate([k, v, v, k], axis=-1).reshape(a.shape[:-1] + (KV_HEADS * KV_PACK,))


def kernel(x, positions, ln_g, ln_b, a_w_in, a_conv_w, a_conv_b, a_dt_bias, a_log, a_d, a_norm_w, a_w_out, kv_w, kv_b, b_w_in, b_q_bias, b_sinks, b_w_out):
    x2 = x.reshape(TOKENS, D_MODEL)

    w_in = a_w_in[0]
    w_main = w_in[:, :ZX_WIDTH].astype(BF16)
    w_dt = jnp.pad(w_in[:, ZX_WIDTH:], ((0, 0), (0, LANES - SSM_HEADS))).astype(BF16)
    zx, dt_raw = _in_proj(x2, w_main, w_dt)

    pad_heads = lambda a: jnp.pad(a, (0, LANES - SSM_HEADS)).reshape(1, LANES)
    y = _ssd(zx, dt_raw, a_conv_w[0], a_conv_b[0].reshape(1, CONV_DIM),
             pad_heads(a_dt_bias[0]), pad_heads(a_log[0]),
             jnp.repeat(a_d[0], SSM_HEAD_DIM).reshape(1, D_INNER),
             a_norm_w[0].reshape(1, D_INNER), _expand_matrix())
    x1 = _out_ln(y, a_w_out[0].astype(BF16), x2, ln_g[0].reshape(1, D_MODEL),
                 ln_b[0].reshape(1, D_MODEL), 256, "ssd_out_ln")

    w_qgk = jnp.concatenate([b_w_in[0], _pack_kv(kv_w)], axis=1).astype(BF16)
    bias = jnp.concatenate([b_q_bias[0], jnp.zeros((Q_WIDTH,), F32), _pack_kv(kv_b)]).reshape(1, QGK_WIDTH)
    inv_freq = ROPE_THETA ** (-jnp.arange(0, ROT_DIM, 2, dtype=F32) / ROT_DIM)
    invf = jnp.tile(inv_freq, LANES // ROPE_SHIFT).reshape(1, LANES)
    pos_b = jnp.broadcast_to(positions.reshape(TOKENS, 1), (TOKENS, LANES))
    qgk = _qgk_proj(x1, pos_b, invf, w_qgk, bias, _rope_masks())
    o = _attention(qgk, b_sinks[0])
    out = _out_ln(o, b_w_out[0].astype(BF16), x1, ln_g[1].reshape(1, D_MODEL),
                  ln_b[1].reshape(1, D_MODEL), 512, "swa_out_ln")
    return out.reshape(BATCH, SEQ, D_MODEL)
```

```python
import math

import numpy as np
import jax
import jax.numpy as jnp
from jax import lax
from jax.experimental import pallas as pl
from jax.experimental.pallas import tpu as pltpu

D_MODEL = 2048
BATCH = 8
SEQ = 4096
TOKENS = BATCH * SEQ
DEPTH = 2

D_INNER = 4096
SSM_HEAD_DIM = 64
SSM_HEADS = 64
SSM_GROUPS = 8
HEADS_PER_GROUP = SSM_HEADS // SSM_GROUPS
GROUP_WIDTH = D_INNER // SSM_GROUPS
D_STATE = 128
D_CONV = 4
CHUNK = 128
GN = SSM_GROUPS * D_STATE
CONV_DIM = D_INNER + 2 * GN
ZX_WIDTH = D_INNER + CONV_DIM

ATT_HEAD_DIM = 64
ATT_HEADS = 32
KV_HEADS = 4
Q_PER_KV = 8
Q_WIDTH = 2048
KV_WIDTH = 256
WINDOW = 128
ROT_DIM = 16
ROPE_SHIFT = ROT_DIM // 2
ROPE_THETA = 500000.0
KV_PACK = 4 * ATT_HEAD_DIM
QGK_WIDTH = 2 * Q_WIDTH + KV_HEADS * KV_PACK

ALPHA = (2.0 * DEPTH) ** 0.25
EPS = 1e-5
LOG2E = math.log2(math.e)

LANES = 128
SUBLANES = 8
BF16_SUBLANES = 16
VMEM_LIMIT_BYTES = 56 * 1024 * 1024

F32 = jnp.float32
BF16 = jnp.bfloat16


def _silu(v):
    h = 0.5 * v
    return h + h * jnp.tanh(h)


def _split_bf16(v, terms):
    parts = []
    rem = v
    for _ in range(terms):
        p = rem.astype(BF16)
        parts.append(p)
        rem = rem - p.astype(F32)
    return parts


PROJ_BM = 1024
PROJ_BN = 1024
PROJ_SUB = 256


def _in_proj_kernel(x_ref, w_ref, wdt_ref, zx_ref, dt_ref, xb_ref):
    @pl.when(pl.program_id(1) == 0)
    def _():
        xb_ref[...] = x_ref[...].astype(BF16)
        dt_ref[...] = jnp.dot(xb_ref[...], wdt_ref[...], preferred_element_type=F32)

    zx_ref[...] = jnp.dot(xb_ref[...], w_ref[...], preferred_element_type=F32).astype(BF16)


def _in_proj(x2, w_main, w_dt):
    grid = (TOKENS // PROJ_BM, ZX_WIDTH // PROJ_BN)
    return pl.pallas_call(
        _in_proj_kernel,
        grid=grid,
        in_specs=[
            pl.BlockSpec((PROJ_BM, D_MODEL), lambda i, j: (i, 0)),
            pl.BlockSpec((D_MODEL, PROJ_BN), lambda i, j: (0, j)),
            pl.BlockSpec((D_MODEL, LANES), lambda i, j: (0, 0)),
        ],
        out_specs=[
            pl.BlockSpec((PROJ_BM, PROJ_BN), lambda i, j: (i, j)),
            pl.BlockSpec((PROJ_BM, LANES), lambda i, j: (i, 0)),
        ],
        out_shape=[
            jax.ShapeDtypeStruct((TOKENS, ZX_WIDTH), BF16),
            jax.ShapeDtypeStruct((TOKENS, LANES), F32),
        ],
        scratch_shapes=[pltpu.VMEM((PROJ_BM, D_MODEL), BF16)],
        compiler_params=pltpu.CompilerParams(
            dimension_semantics=("arbitrary", "arbitrary"),
            vmem_limit_bytes=VMEM_LIMIT_BYTES),
        name="in_proj",
    )(x2, w_main, w_dt)


XS_HALF = D_INNER // 2
EXPAND_ROWS = 2 * CHUNK + BF16_SUBLANES


def _ssd_kernel(z_ref, xa_ref, xb_ref, bc_ref, dt_ref, cw_ref, cb_ref, dtb_ref, alog_ref,
                dsk_ref, nw_ref, e2_ref, y_ref, state_ref, stage_ref):
    @pl.when(pl.program_id(1) == 0)
    def _():
        state_ref[...] = jnp.zeros_like(state_ref)
        stage_ref[0:SUBLANES, :] = jnp.zeros((SUBLANES, CONV_DIM), F32)

    for k, src in enumerate((xa_ref, xb_ref, bc_ref)):
        stage_ref[SUBLANES:SUBLANES + CHUNK, k * XS_HALF:(k + 1) * XS_HALF] = src[...].astype(F32)

    def conv_silu(lo, width):
        acc = cb_ref[:, lo:lo + width]
        for j in range(D_CONV):
            acc = acc + (stage_ref[SUBLANES - j:SUBLANES - j + CHUNK, lo:lo + width]
                         * cw_ref[D_CONV - 1 - j:D_CONV - j, lo:lo + width])
        return _silu(acc)

    v = dt_ref[...] + dtb_ref[...]
    dt = jnp.maximum(v, 0.0) + jnp.log1p(jnp.exp(-jnp.abs(v)))
    da = dt * (-jnp.exp(alog_ref[...]))
    rq = lax.broadcasted_iota(jnp.int32, (CHUNK, CHUNK), 0)
    cs = lax.broadcasted_iota(jnp.int32, (CHUNK, CHUNK), 1)
    tri = rq >= cs
    cum3 = jnp.dot(jnp.where(tri, 1.0, 0.0).astype(BF16),
                   jnp.concatenate(_split_bf16(da, 3), axis=1), preferred_element_type=F32)
    cum = cum3[:, 0:LANES] + cum3[:, LANES:2 * LANES] + cum3[:, 2 * LANES:3 * LANES]
    c2 = cum * LOG2E
    c2_src_t = (c2 - jnp.log2(dt)).T
    last = cum[CHUNK - 1:CHUNK, :]
    stack = jnp.concatenate(
        [jnp.exp(cum), dt * jnp.exp(last - cum),
         jnp.broadcast_to(jnp.exp(last), (BF16_SUBLANES, LANES))], axis=0)
    expand_lhs = jnp.concatenate(_split_bf16(stack, 2), axis=1)

    lane = lax.broadcasted_iota(jnp.int32, (CHUNK, LANES), 1)
    keep_lo = jnp.where(lane < SSM_HEAD_DIM, 1.0, 0.0).astype(BF16)
    keep_hi = jnp.where(lane >= SSM_HEAD_DIM, 1.0, 0.0).astype(BF16)

    for g in range(SSM_GROUPS):
        gs = slice(g * GROUP_WIDTH, (g + 1) * GROUP_WIDTH)
        ex = jnp.dot(expand_lhs, e2_ref[:, gs], preferred_element_type=F32)
        ecum_x = ex[0:CHUNK]
        dstate_x = ex[CHUNK:2 * CHUNK]
        elast_x = ex[2 * CHUNK:2 * CHUNK + 1]

        xs = conv_silu(g * GROUP_WIDTH, GROUP_WIDTH)
        bm = conv_silu(D_INNER + g * D_STATE, D_STATE)
        cm = conv_silu(D_INNER + GN + g * D_STATE, D_STATE)

        bb = bm.astype(BF16)
        cb = cm.astype(BF16)
        cbt = lax.dot_general(cb, bb, (((1,), (1,)), ((), ())),
                              preferred_element_type=F32).astype(BF16)

        s_prev = state_ref[:, gs]
        y_off = jnp.dot(cb, s_prev.astype(BF16), preferred_element_type=F32) * ecum_x
        state_ref[:, gs] = s_prev * elast_x + jnp.dot(
            bm.T.astype(BF16), (xs * dstate_x).astype(BF16), preferred_element_type=F32)

        xsb = xs.astype(BF16)
        pairs = []
        for p in range(HEADS_PER_GROUP // 2):
            xp = xsb[:, p * LANES:(p + 1) * LANES]
            rhs = jnp.concatenate([xp * keep_lo, xp * keep_hi], axis=0)
            ms = []
            for half in range(2):
                h = g * HEADS_PER_GROUP + 2 * p + half
                decay = jnp.where(tri, jnp.exp2(c2[:, h:h + 1] - c2_src_t[h:h + 1, :]), 0.0)
                ms.append(cbt * decay.astype(BF16))
            pairs.append(jnp.dot(jnp.concatenate(ms, axis=1), rhs, preferred_element_type=F32))
        y = jnp.concatenate(pairs, axis=1) + y_off + xs * dsk_ref[:, gs]
        y = y * _silu(z_ref[:, gs].astype(F32))
        ms = jnp.mean(y * y, axis=-1, keepdims=True)
        y_ref[:, gs] = (y * lax.rsqrt(ms + EPS) * nw_ref[:, gs]).astype(BF16)

    stage_ref[0:SUBLANES, :] = stage_ref[CHUNK:CHUNK + SUBLANES, :]


def _ssd(zx, dt_raw, conv_w, conv_b, dt_bias, a_log, d_skip_x, norm_w, e2):
    nc = SEQ // CHUNK
    row = lambda b, c: b * nc + c
    const = lambda b, c: (0, 0)
    zx_cols = ZX_WIDTH // XS_HALF
    assert zx_cols == 5
    return pl.pallas_call(
        _ssd_kernel,
        grid=(BATCH, nc),
        in_specs=[
            pl.BlockSpec((CHUNK, D_INNER), lambda b, c: (row(b, c), 0)),
            pl.BlockSpec((CHUNK, XS_HALF), lambda b, c: (row(b, c), 2)),
            pl.BlockSpec((CHUNK, XS_HALF), lambda b, c: (row(b, c), 3)),
            pl.BlockSpec((CHUNK, XS_HALF), lambda b, c: (row(b, c), 4)),
            pl.BlockSpec((CHUNK, LANES), lambda b, c: (row(b, c), 0)),
            pl.BlockSpec((D_CONV, CONV_DIM), const),
            pl.BlockSpec((1, CONV_DIM), const),
            pl.BlockSpec((1, LANES), const),
            pl.BlockSpec((1, LANES), const),
            pl.BlockSpec((1, D_INNER), const),
            pl.BlockSpec((1, D_INNER), const),
            pl.BlockSpec((2 * LANES, D_INNER), const),
        ],
        out_specs=pl.BlockSpec((CHUNK, D_INNER), lambda b, c: (row(b, c), 0)),
        out_shape=jax.ShapeDtypeStruct((TOKENS, D_INNER), BF16),
        scratch_shapes=[
            pltpu.VMEM((D_STATE, D_INNER), F32),
            pltpu.VMEM((SUBLANES + CHUNK, CONV_DIM), F32),
        ],
        compiler_params=pltpu.CompilerParams(
            dimension_semantics=("arbitrary", "arbitrary"),
            vmem_limit_bytes=VMEM_LIMIT_BYTES),
        name="ssd",
    )(zx, zx, zx, zx, dt_raw, conv_w, conv_b, dt_bias, a_log, d_skip_x, norm_w, e2)


def _out_ln_kernel(y_ref, w_ref, x_ref, g_ref, b_ref, o_ref):
    h = jnp.dot(y_ref[...], w_ref[...], preferred_element_type=F32)
    v = ALPHA * x_ref[...] + h
    mu = jnp.mean(v, axis=-1, keepdims=True)
    d = v - mu
    var = jnp.mean(d * d, axis=-1, keepdims=True)
    o_ref[...] = d * lax.rsqrt(var + EPS) * g_ref[...] + b_ref[...]


def _out_ln(y, w, x2, g, b, bm, name):
    k = y.shape[1]
    return pl.pallas_call(
        _out_ln_kernel,
        grid=(TOKENS // bm,),
        in_specs=[
            pl.BlockSpec((bm, k), lambda i: (i, 0)),
            pl.BlockSpec((k, D_MODEL), lambda i: (0, 0), pipeline_mode=pl.Buffered(1)),
            pl.BlockSpec((bm, D_MODEL), lambda i: (i, 0)),
            pl.BlockSpec((1, D_MODEL), lambda i: (0, 0)),
            pl.BlockSpec((1, D_MODEL), lambda i: (0, 0)),
        ],
        out_specs=pl.BlockSpec((bm, D_MODEL), lambda i: (i, 0)),
        out_shape=jax.ShapeDtypeStruct((TOKENS, D_MODEL), F32),
        compiler_params=pltpu.CompilerParams(
            dimension_semantics=("arbitrary",),
            vmem_limit_bytes=VMEM_LIMIT_BYTES),
        name=name,
    )(y, w, x2, g, b)


TABLE_ROWS = TOKENS * ROPE_SHIFT // LANES


def _rope_table_kernel(pos_ref, invf_ref, cos_ref, sin_ref):
    ang = pos_ref[...].astype(F32) * invf_ref[...]
    cos_ref[...] = jnp.cos(ang)
    sin_ref[...] = jnp.sin(ang)


def _rope_tables(pos_rep, invf):
    return pl.pallas_call(
        _rope_table_kernel,
        out_shape=[jax.ShapeDtypeStruct((TABLE_ROWS, LANES), F32)] * 2,
        compiler_params=pltpu.CompilerParams(vmem_limit_bytes=VMEM_LIMIT_BYTES),
        name="rope_tables",
    )(pos_rep, invf)


QGK_BM = 1024
ROPE_PERIOD = KV_PACK
Q_TILES = Q_WIDTH // PROJ_BN


def _qgk_kernel(x_ref, cos_ref, sin_ref, w_ref, bias_ref, mask_ref, o_ref, xb_ref, tab_ref):
    j = pl.program_id(1)

    @pl.when(j == 0)
    def _():
        xb_ref[...] = x_ref[...].astype(BF16)
        reps = ROPE_PERIOD // LANES
        cos_p = jnp.tile(cos_ref[...], (1, reps))
        sin_p = jnp.tile(sin_ref[...], (1, reps))
        for t in range(2):
            m = mask_ref[t]
            tab_ref[t, 0] = 1.0 + m[0:1] * (cos_p - 1.0)
            tab_ref[t, 1] = m[1:2] * sin_p
            tab_ref[t, 2] = m[2:3] * sin_p

    has_rope = (j < Q_TILES) | (j >= 2 * Q_TILES)

    @pl.when(has_rope)
    def _():
        t = jnp.where(j >= 2 * Q_TILES, 1, 0)
        reps = PROJ_BN // ROPE_PERIOD
        for r in range(QGK_BM // PROJ_SUB):
            rows = slice(r * PROJ_SUB, (r + 1) * PROJ_SUB)
            acc = jnp.dot(xb_ref[rows, :], w_ref[...], preferred_element_type=F32) + bias_ref[...]
            upper = pltpu.roll(acc, PROJ_BN - ROPE_SHIFT, axis=1)
            lower = pltpu.roll(acc, ROPE_SHIFT, axis=1)
            out = (acc * jnp.tile(tab_ref[t, 0, rows, :], (1, reps))
                   + upper * jnp.tile(tab_ref[t, 1, rows, :], (1, reps))
                   + lower * jnp.tile(tab_ref[t, 2, rows, :], (1, reps)))
            o_ref[rows, :] = out.astype(BF16)

    @pl.when(jnp.logical_not(has_rope))
    def _():
        o_ref[...] = (jnp.dot(xb_ref[...], w_ref[...], preferred_element_type=F32)
                      + bias_ref[...]).astype(BF16)


def _qgk_proj(x1, cos_t, sin_t, w, bias, masks):
    grid = (TOKENS // QGK_BM, QGK_WIDTH // PROJ_BN)
    return pl.pallas_call(
        _qgk_kernel,
        grid=grid,
        in_specs=[
            pl.BlockSpec((QGK_BM, D_MODEL), lambda i, j: (i, 0)),
            pl.BlockSpec((QGK_BM, LANES), lambda i, j: (i, 0)),
            pl.BlockSpec((QGK_BM, LANES), lambda i, j: (i, 0)),
            pl.BlockSpec((D_MODEL, PROJ_BN), lambda i, j: (0, j)),
            pl.BlockSpec((1, PROJ_BN), lambda i, j: (0, j)),
            pl.BlockSpec((2, SUBLANES, ROPE_PERIOD), lambda i, j: (0, 0, 0)),
        ],
        out_specs=pl.BlockSpec((QGK_BM, PROJ_BN), lambda i, j: (i, j)),
        out_shape=jax.ShapeDtypeStruct((TOKENS, QGK_WIDTH), BF16),
        scratch_shapes=[
            pltpu.VMEM((QGK_BM, D_MODEL), BF16),
            pltpu.VMEM((2, 3, QGK_BM, ROPE_PERIOD), F32),
        ],
        compiler_params=pltpu.CompilerParams(
            dimension_semantics=("arbitrary", "arbitrary"),
            vmem_limit_bytes=VMEM_LIMIT_BYTES),
        name="qgk_proj",
    )(x1, cos_t, sin_t, w, bias, masks)


PAIRS_PER_KV = Q_PER_KV // 2
STACK_ROWS = PAIRS_PER_KV * WINDOW
KV_GROUP_Q = Q_PER_KV * ATT_HEAD_DIM


def _attn_kernel(sink_ref, q_ref, gate_ref, kvp_ref, kvc_ref, o_ref):
    c = pl.program_id(1)
    row = lax.broadcasted_iota(jnp.int32, (STACK_ROWS, 2 * WINDOW), 0)
    col = lax.broadcasted_iota(jnp.int32, (STACK_ROWS, 2 * WINDOW), 1)
    qi = row & (WINDOW - 1)
    valid = (col > qi) & (col <= qi + WINDOW) & ((col >= WINDOW) | (c > 0))
    lane = lax.broadcasted_iota(jnp.int32, (STACK_ROWS, LANES), 1)
    low_half = lane < ATT_HEAD_DIM
    pair_of_row = lax.broadcasted_iota(jnp.int32, (STACK_ROWS, 1), 0) // WINDOW
    scale = ATT_HEAD_DIM ** -0.5

    for hk in range(KV_HEADS):
        ks = slice(hk * KV_PACK, (hk + 1) * KV_PACK)
        kv = jnp.concatenate([kvp_ref[:, ks], kvc_ref[:, ks]], axis=0)
        kv_kv = kv[:, 0:LANES]
        kv_vk = kv[:, LANES:2 * LANES]
        qs = jnp.concatenate(
            [q_ref[:, hk * KV_GROUP_Q + p * LANES:hk * KV_GROUP_Q + (p + 1) * LANES]
             for p in range(PAIRS_PER_KV)], axis=0)
        outs = []
        for half in range(2):
            qm = jnp.where(low_half if half == 0 else ~low_half, qs, jnp.zeros_like(qs))
            keys = kv_kv if half == 0 else kv_vk
            s = lax.dot_general(qm, keys, (((1,), (1,)), ((), ())), preferred_element_type=F32) * scale
            s = jnp.where(valid, s, -jnp.inf)
            sk = jnp.zeros((STACK_ROWS, 1), F32)
            for p in range(PAIRS_PER_KV):
                sk = jnp.where(pair_of_row == p, sink_ref[hk * Q_PER_KV + 2 * p + half], sk)
            m = jnp.maximum(jnp.max(s, axis=-1, keepdims=True), sk)
            pe = jnp.exp(s - m)
            den = jnp.sum(pe, axis=-1, keepdims=True) + jnp.exp(sk - m)
            vals = kv_vk if half == 0 else kv_kv
            outs.append(jnp.dot(pe.astype(BF16), vals, preferred_element_type=F32) / den)
        o_pairs = jnp.where(low_half, outs[0], outs[1])
        for p in range(PAIRS_PER_KV):
            cols = slice(hk * KV_GROUP_Q + p * LANES, hk * KV_GROUP_Q + (p + 1) * LANES)
            o_ref[:, cols] = (o_pairs[p * WINDOW:(p + 1) * WINDOW]
                              * _silu(gate_ref[:, cols].astype(F32))).astype(BF16)


def _attention(qgk, sinks):
    nb = SEQ // WINDOW
    row = lambda b, c: b * nb + c
    kv_col = (2 * Q_WIDTH) // (KV_HEADS * KV_PACK)
    return pl.pallas_call(
        _attn_kernel,
        grid=(BATCH, nb),
        in_specs=[
            pl.BlockSpec(memory_space=pltpu.SMEM),
            pl.BlockSpec((WINDOW, Q_WIDTH), lambda b, c: (row(b, c), 0)),
            pl.BlockSpec((WINDOW, Q_WIDTH), lambda b, c: (row(b, c), 1)),
            pl.BlockSpec((WINDOW, KV_HEADS * KV_PACK), lambda b, c: (row(b, jnp.maximum(c - 1, 0)), kv_col)),
            pl.BlockSpec((WINDOW, KV_HEADS * KV_PACK), lambda b, c: (row(b, c), kv_col)),
        ],
        out_specs=pl.BlockSpec((WINDOW, Q_WIDTH), lambda b, c: (row(b, c), 0)),
        out_shape=jax.ShapeDtypeStruct((TOKENS, Q_WIDTH), BF16),
        compiler_params=pltpu.CompilerParams(
            dimension_semantics=("arbitrary", "arbitrary"),
            vmem_limit_bytes=VMEM_LIMIT_BYTES),
        name="swa_attention",
    )(sinks, qgk, qgk, qgk, qgk)


def _expand_matrix():
    e = np.zeros((2 * LANES, D_INNER), np.float32)
    for h in range(SSM_HEADS):
        e[h, h * SSM_HEAD_DIM:(h + 1) * SSM_HEAD_DIM] = 1.0
        e[LANES + h, h * SSM_HEAD_DIM:(h + 1) * SSM_HEAD_DIM] = 1.0
    return jnp.asarray(e, BF16)


def _rope_masks():
    m = np.zeros((2, SUBLANES, ROPE_PERIOD), np.float32)
    d = np.arange(ATT_HEAD_DIM)
    sect = np.stack([1.0 * (d < ROT_DIM), -1.0 * (d < ROPE_SHIFT),
                     1.0 * ((d >= ROPE_SHIFT) & (d < ROT_DIM))]).astype(np.float32)
    for s in range(ROPE_PERIOD // ATT_HEAD_DIM):
        m[0, 0:3, s * ATT_HEAD_DIM:(s + 1) * ATT_HEAD_DIM] = sect
        if s in (0, 3):
            m[1, 0:3, s * ATT_HEAD_DIM:(s + 1) * ATT_HEAD_DIM] = sect
    return jnp.asarray(m)


def _pack_kv(a):
    k = a[..., :KV_WIDTH].reshape(a.shape[:-1] + (KV_HEADS, ATT_HEAD_DIM))
    v = a[..., KV_WIDTH:].reshape(a.shape[:-1] + (KV_HEADS, ATT_HEAD_DIM))
    return jnp.concatenate([k, v, v, k], axis=-1).reshape(a.shape[:-1] + (KV_HEADS * KV_PACK,))


def kernel(x, positions, ln_g, ln_b, a_w_in, a_conv_w, a_conv_b, a_dt_bias, a_log, a_d, a_norm_w, a_w_out, kv_w, kv_b, b_w_in, b_q_bias, b_sinks, b_w_out):
    x2 = x.reshape(TOKENS, D_MODEL)

    w_in = a_w_in[0]
    w_main = w_in[:, :ZX_WIDTH].astype(BF16)
    w_dt = jnp.pad(w_in[:, ZX_WIDTH:], ((0, 0), (0, LANES - SSM_HEADS))).astype(BF16)
    zx, dt_raw = _in_proj(x2, w_main, w_dt)

    pad_heads = lambda a: jnp.pad(a, (0, LANES - SSM_HEADS)).reshape(1, LANES)
    y = _ssd(zx, dt_raw, a_conv_w[0], a_conv_b[0].reshape(1, CONV_DIM),
             pad_heads(a_dt_bias[0]), pad_heads(a_log[0]),
             jnp.repeat(a_d[0], SSM_HEAD_DIM).reshape(1, D_INNER),
             a_norm_w[0].reshape(1, D_INNER), _expand_matrix())
    x1 = _out_ln(y, a_w_out[0].astype(BF16), x2, ln_g[0].reshape(1, D_MODEL),
                 ln_b[0].reshape(1, D_MODEL), 256, "ssd_out_ln")

    w_qgk = jnp.concatenate([b_w_in[0], _pack_kv(kv_w)], axis=1).astype(BF16)
    bias = jnp.concatenate([b_q_bias[0], jnp.zeros((Q_WIDTH,), F32), _pack_kv(kv_b)]).reshape(1, QGK_WIDTH)
    inv_freq = ROPE_THETA ** (-jnp.arange(0, ROT_DIM, 2, dtype=F32) / ROT_DIM)
    invf = jnp.tile(inv_freq, LANES // ROPE_SHIFT).reshape(1, LANES)
    tokens_per_row = LANES // ROPE_SHIFT
    pos_rep = jnp.repeat(positions.reshape(TABLE_ROWS, tokens_per_row), ROPE_SHIFT, axis=1)
    cos_c, sin_c = _rope_tables(pos_rep, invf)
    cos_t = jnp.tile(cos_c.reshape(TOKENS, ROPE_SHIFT), (1, tokens_per_row))
    sin_t = jnp.tile(sin_c.reshape(TOKENS, ROPE_SHIFT), (1, tokens_per_row))
    qgk = _qgk_proj(x1, cos_t, sin_t, w_qgk, bias, _rope_masks())
    o = _attention(qgk, b_sinks[0])
    out = _out_ln(o, b_w_out[0].astype(BF16), x1, ln_g[1].reshape(1, D_MODEL),
                  ln_b[1].reshape(1, D_MODEL), 512, "swa_out_ln")
    return out.reshape(BATCH, SEQ, D_MODEL)
```

```python
import math

import numpy as np
import jax
import jax.numpy as jnp
from jax import lax
from jax.experimental import pallas as pl
from jax.experimental.pallas import tpu as pltpu

D_MODEL = 2048
BATCH = 8
SEQ = 4096
TOKENS = BATCH * SEQ
DEPTH = 2

D_INNER = 4096
SSM_HEAD_DIM = 64
SSM_HEADS = 64
SSM_GROUPS = 8
HEADS_PER_GROUP = SSM_HEADS // SSM_GROUPS
GROUP_WIDTH = D_INNER // SSM_GROUPS
D_STATE = 128
D_CONV = 4
CHUNK = 128
GN = SSM_GROUPS * D_STATE
CONV_DIM = D_INNER + 2 * GN
ZX_WIDTH = D_INNER + CONV_DIM

ATT_HEAD_DIM = 64
ATT_HEADS = 32
KV_HEADS = 4
Q_PER_KV = 8
Q_WIDTH = 2048
KV_WIDTH = 256
WINDOW = 128
ROT_DIM = 16
ROPE_SHIFT = ROT_DIM // 2
ROPE_THETA = 500000.0
KV_PACK = 4 * ATT_HEAD_DIM
QGK_WIDTH = 2 * Q_WIDTH + KV_HEADS * KV_PACK

ALPHA = (2.0 * DEPTH) ** 0.25
EPS = 1e-5
LOG2E = math.log2(math.e)

LANES = 128
SUBLANES = 8
BF16_SUBLANES = 16
VMEM_LIMIT_BYTES = 56 * 1024 * 1024

F32 = jnp.float32
BF16 = jnp.bfloat16


def _silu(v):
    h = 0.5 * v
    return h + h * jnp.tanh(h)


def _split_bf16(v, terms):
    parts = []
    rem = v
    for _ in range(terms):
        p = rem.astype(BF16)
        parts.append(p)
        rem = rem - p.astype(F32)
    return parts


PROJ_BM = 1024
PROJ_BN = 1024
PROJ_SUB = 256
PROJ_TILES = ZX_WIDTH // PROJ_BN
Z_TILES = D_INNER // PROJ_BN


def _in_proj_kernel(x_ref, w_ref, wdt_ref, zx_ref, dt_ref, xb_ref):
    j = pl.program_id(1)

    @pl.when(j == 0)
    def _():
        xb_ref[...] = x_ref[...].astype(BF16)
        dt_ref[...] = jnp.dot(xb_ref[...], wdt_ref[...], preferred_element_type=F32)

    @pl.when(j < Z_TILES)
    def _():
        for r in range(PROJ_BM // PROJ_SUB):
            rows = slice(r * PROJ_SUB, (r + 1) * PROJ_SUB)
            acc = jnp.dot(xb_ref[rows, :], w_ref[...], preferred_element_type=F32)
            zx_ref[rows, :] = _silu(acc).astype(BF16)

    @pl.when(j >= Z_TILES)
    def _():
        zx_ref[...] = jnp.dot(xb_ref[...], w_ref[...], preferred_element_type=F32).astype(BF16)


def _in_proj(x2, w_in, w_dt):
    grid = (TOKENS // PROJ_BM, PROJ_TILES)
    return pl.pallas_call(
        _in_proj_kernel,
        grid=grid,
        in_specs=[
            pl.BlockSpec((PROJ_BM, D_MODEL), lambda i, j: (i, 0)),
            pl.BlockSpec((D_MODEL, PROJ_BN), lambda i, j: (0, j)),
            pl.BlockSpec((D_MODEL, LANES), lambda i, j: (0, 0)),
        ],
        out_specs=[
            pl.BlockSpec((PROJ_BM, PROJ_BN), lambda i, j: (i, j)),
            pl.BlockSpec((PROJ_BM, LANES), lambda i, j: (i, 0)),
        ],
        out_shape=[
            jax.ShapeDtypeStruct((TOKENS, ZX_WIDTH), BF16),
            jax.ShapeDtypeStruct((TOKENS, LANES), F32),
        ],
        scratch_shapes=[pltpu.VMEM((PROJ_BM, D_MODEL), BF16)],
        compiler_params=pltpu.CompilerParams(
            dimension_semantics=("arbitrary", "arbitrary"),
            vmem_limit_bytes=VMEM_LIMIT_BYTES),
        name="in_proj",
    )(x2, w_in, w_dt)


XS_HALF = D_INNER // 2
EXPAND_ROWS = 2 * CHUNK + BF16_SUBLANES
CONV_SLABS = CONV_DIM // LANES
CONV_STRIDE = 4


def _ssd_kernel(z_ref, xa_ref, xb_ref, bc_ref, dt_ref, cw_ref, cb_ref, dtb_ref, alog_ref,
                dsk_ref, nw_ref, e2_ref, y_ref, state_ref, stage_ref, conv_ref):
    @pl.when(pl.program_id(1) == 0)
    def _():
        state_ref[...] = jnp.zeros_like(state_ref)
        for s in range(CONV_SLABS):
            stage_ref[s, 0:SUBLANES, :] = jnp.zeros((SUBLANES, LANES), F32)

    for k3, src in enumerate((xa_ref, xb_ref, bc_ref)):
        for s in range(XS_HALF // LANES):
            slab = k3 * (XS_HALF // LANES) + s
            stage_ref[slab, SUBLANES:SUBLANES + CHUNK, :] = src[:, s * LANES:(s + 1) * LANES].astype(F32)
    per_residue = CHUNK // CONV_STRIDE
    for slab in range(CONV_SLABS):
        cols = slice(slab * LANES, (slab + 1) * LANES)
        taps = [cw_ref[D_CONV - 1 - j:D_CONV - j, cols] for j in range(D_CONV)]
        bias = cb_ref[:, cols]
        for k in range(CONV_STRIDE):
            acc = bias
            for j in range(D_CONV):
                acc = acc + stage_ref[slab, pl.ds(SUBLANES + k - j, per_residue, stride=CONV_STRIDE), :] * taps[j]
            conv_ref[slab, pl.ds(k, per_residue, stride=CONV_STRIDE), :] = _silu(acc)
        stage_ref[slab, 0:SUBLANES, :] = stage_ref[slab, CHUNK:CHUNK + SUBLANES, :]

    v = dt_ref[...] + dtb_ref[...]
    dt = jnp.maximum(v, 0.0) + jnp.log1p(jnp.exp(-jnp.abs(v)))
    da = dt * (-jnp.exp(alog_ref[...]))
    rq = lax.broadcasted_iota(jnp.int32, (CHUNK, CHUNK), 0)
    cs = lax.broadcasted_iota(jnp.int32, (CHUNK, CHUNK), 1)
    tri = rq >= cs
    cum3 = jnp.dot(jnp.where(tri, 1.0, 0.0).astype(BF16),
                   jnp.concatenate(_split_bf16(da, 3), axis=1), preferred_element_type=F32)
    cum = cum3[:, 0:LANES] + cum3[:, LANES:2 * LANES] + cum3[:, 2 * LANES:3 * LANES]
    c2 = cum * LOG2E
    c2_src_t = (c2 - jnp.log2(dt)).T
    last = cum[CHUNK - 1:CHUNK, :]
    stack = jnp.concatenate(
        [jnp.exp(cum), dt * jnp.exp(last - cum),
         jnp.broadcast_to(jnp.exp(last), (BF16_SUBLANES, LANES))], axis=0)
    expand_lhs = jnp.concatenate(_split_bf16(stack, 2), axis=1)

    lane = lax.broadcasted_iota(jnp.int32, (CHUNK, LANES), 1)
    keep_lo = jnp.where(lane < SSM_HEAD_DIM, 1.0, 0.0).astype(BF16)
    keep_hi = jnp.where(lane >= SSM_HEAD_DIM, 1.0, 0.0).astype(BF16)

    for g in range(SSM_GROUPS):
        gs = slice(g * GROUP_WIDTH, (g + 1) * GROUP_WIDTH)
        ex = jnp.dot(expand_lhs, e2_ref[:, gs], preferred_element_type=F32)
        ecum_x = ex[0:CHUNK]
        dstate_x = ex[CHUNK:2 * CHUNK]
        elast_x = ex[2 * CHUNK:2 * CHUNK + 1]

        slabs_per_group = GROUP_WIDTH // LANES
        xs = jnp.concatenate([conv_ref[g * slabs_per_group + q] for q in range(slabs_per_group)], axis=1)
        xsb = xs.astype(BF16)
        bm = conv_ref[D_INNER // LANES + g]
        bb = bm.astype(BF16)
        cb = conv_ref[(D_INNER + GN) // LANES + g].astype(BF16)
        cbt = lax.dot_general(cb, bb, (((1,), (1,)), ((), ())),
                              preferred_element_type=F32).astype(BF16)

        s_prev = state_ref[:, gs]
        y_off = jnp.dot(cb, s_prev.astype(BF16), preferred_element_type=F32) * ecum_x
        state_ref[:, gs] = s_prev * elast_x + jnp.dot(
            bm.T.astype(BF16), (xs * dstate_x).astype(BF16), preferred_element_type=F32)

        pairs = []
        for p in range(HEADS_PER_GROUP // 2):
            xp = xsb[:, p * LANES:(p + 1) * LANES]
            rhs = jnp.concatenate([xp * keep_lo, xp * keep_hi], axis=0)
            ms = []
            for half in range(2):
                h = g * HEADS_PER_GROUP + 2 * p + half
                decay = jnp.where(tri, jnp.exp2(c2[:, h:h + 1] - c2_src_t[h:h + 1, :]), 0.0)
                ms.append(cbt * decay.astype(BF16))
            pairs.append(jnp.dot(jnp.concatenate(ms, axis=1), rhs, preferred_element_type=F32))
        y = jnp.concatenate(pairs, axis=1) + y_off + xs * dsk_ref[:, gs]
        y = y * z_ref[:, gs].astype(F32)
        ms = jnp.mean(y * y, axis=-1, keepdims=True)
        y_ref[:, gs] = (y * lax.rsqrt(ms + EPS) * nw_ref[:, gs]).astype(BF16)


def _ssd(zx, dt_raw, conv_w, conv_b, dt_bias, a_log, d_skip_x, norm_w, e2):
    nc = SEQ // CHUNK
    row = lambda b, c: b * nc + c
    const = lambda b, c: (0, 0)
    zx_cols = ZX_WIDTH // XS_HALF
    assert zx_cols == 5
    return pl.pallas_call(
        _ssd_kernel,
        grid=(BATCH, nc),
        in_specs=[
            pl.BlockSpec((CHUNK, D_INNER), lambda b, c: (row(b, c), 0)),
            pl.BlockSpec((CHUNK, XS_HALF), lambda b, c: (row(b, c), 2)),
            pl.BlockSpec((CHUNK, XS_HALF), lambda b, c: (row(b, c), 3)),
            pl.BlockSpec((CHUNK, XS_HALF), lambda b, c: (row(b, c), 4)),
            pl.BlockSpec((CHUNK, LANES), lambda b, c: (row(b, c), 0)),
            pl.BlockSpec((D_CONV, CONV_DIM), const),
            pl.BlockSpec((1, CONV_DIM), const),
            pl.BlockSpec((1, LANES), const),
            pl.BlockSpec((1, LANES), const),
            pl.BlockSpec((1, D_INNER), const),
            pl.BlockSpec((1, D_INNER), const),
            pl.BlockSpec((2 * LANES, D_INNER), const),
        ],
        out_specs=pl.BlockSpec((CHUNK, D_INNER), lambda b, c: (row(b, c), 0)),
        out_shape=jax.ShapeDtypeStruct((TOKENS, D_INNER), BF16),
        scratch_shapes=[
            pltpu.VMEM((D_STATE, D_INNER), F32),
            pltpu.VMEM((CONV_SLABS, SUBLANES + CHUNK, LANES), F32),
            pltpu.VMEM((CONV_SLABS, CHUNK, LANES), F32),
        ],
        compiler_params=pltpu.CompilerParams(
            dimension_semantics=("arbitrary", "arbitrary"),
            vmem_limit_bytes=VMEM_LIMIT_BYTES),
        name="ssd",
    )(zx, zx, zx, zx, dt_raw, conv_w, conv_b, dt_bias, a_log, d_skip_x, norm_w, e2)


def _out_ln_kernel(y_ref, w_ref, x_ref, g_ref, b_ref, o_ref):
    h = jnp.dot(y_ref[...], w_ref[...], preferred_element_type=F32)
    v = ALPHA * x_ref[...] + h
    mu = jnp.mean(v, axis=-1, keepdims=True)
    d = v - mu
    var = jnp.mean(d * d, axis=-1, keepdims=True)
    o_ref[...] = d * lax.rsqrt(var + EPS) * g_ref[...] + b_ref[...]


def _out_ln(y, w, x2, g, b, bm, name):
    k = y.shape[1]
    return pl.pallas_call(
        _out_ln_kernel,
        grid=(TOKENS // bm,),
        in_specs=[
            pl.BlockSpec((bm, k), lambda i: (i, 0)),
            pl.BlockSpec((k, D_MODEL), lambda i: (0, 0), pipeline_mode=pl.Buffered(1)),
            pl.BlockSpec((bm, D_MODEL), lambda i: (i, 0)),
            pl.BlockSpec((1, D_MODEL), lambda i: (0, 0)),
            pl.BlockSpec((1, D_MODEL), lambda i: (0, 0)),
        ],
        out_specs=pl.BlockSpec((bm, D_MODEL), lambda i: (i, 0)),
        out_shape=jax.ShapeDtypeStruct((TOKENS, D_MODEL), F32),
        compiler_params=pltpu.CompilerParams(
            dimension_semantics=("arbitrary",),
            vmem_limit_bytes=VMEM_LIMIT_BYTES),
        name=name,
    )(y, w, x2, g, b)


TABLE_ROWS = TOKENS * ROPE_SHIFT // LANES


def _rope_table_kernel(pos_ref, invf_ref, cos_ref, sin_ref):
    ang = pos_ref[...].astype(F32) * invf_ref[...]
    cos_ref[...] = jnp.cos(ang)
    sin_ref[...] = jnp.sin(ang)


def _rope_tables(pos_rep, invf):
    return pl.pallas_call(
        _rope_table_kernel,
        out_shape=[jax.ShapeDtypeStruct((TABLE_ROWS, LANES), F32)] * 2,
        compiler_params=pltpu.CompilerParams(vmem_limit_bytes=VMEM_LIMIT_BYTES),
        name="rope_tables",
    )(pos_rep, invf)


QGK_BM = 1024
ROPE_PERIOD = KV_PACK
Q_TILES = Q_WIDTH // PROJ_BN


def _qgk_kernel(x_ref, cos_ref, sin_ref, w_ref, bias_ref, mask_ref, o_ref, xb_ref, tab_ref):
    j = pl.program_id(1)

    @pl.when(j == 0)
    def _():
        xb_ref[...] = x_ref[...].astype(BF16)
        reps = ROPE_PERIOD // LANES
        cos_p = jnp.tile(cos_ref[...], (1, reps))
        sin_p = jnp.tile(sin_ref[...], (1, reps))
        for t in range(2):
            m = mask_ref[t]
            tab_ref[t, 0] = 1.0 + m[0:1] * (cos_p - 1.0)
            tab_ref[t, 1] = m[1:2] * sin_p
            tab_ref[t, 2] = m[2:3] * sin_p

    has_rope = (j < Q_TILES) | (j >= 2 * Q_TILES)

    @pl.when(has_rope)
    def _():
        t = jnp.where(j >= 2 * Q_TILES, 1, 0)
        reps = PROJ_BN // ROPE_PERIOD
        for r in range(QGK_BM // PROJ_SUB):
            rows = slice(r * PROJ_SUB, (r + 1) * PROJ_SUB)
            acc = jnp.dot(xb_ref[rows, :], w_ref[...], preferred_element_type=F32) + bias_ref[...]
            upper = pltpu.roll(acc, PROJ_BN - ROPE_SHIFT, axis=1)
            lower = pltpu.roll(acc, ROPE_SHIFT, axis=1)
            out = (acc * jnp.tile(tab_ref[t, 0, rows, :], (1, reps))
                   + upper * jnp.tile(tab_ref[t, 1, rows, :], (1, reps))
                   + lower * jnp.tile(tab_ref[t, 2, rows, :], (1, reps)))
            o_ref[rows, :] = out.astype(BF16)

    @pl.when(jnp.logical_not(has_rope))
    def _():
        for r in range(QGK_BM // PROJ_SUB):
            rows = slice(r * PROJ_SUB, (r + 1) * PROJ_SUB)
            acc = jnp.dot(xb_ref[rows, :], w_ref[...], preferred_element_type=F32) + bias_ref[...]
            o_ref[rows, :] = _silu(acc).astype(BF16)


def _qgk_proj(x1, cos_t, sin_t, w, bias, masks):
    grid = (TOKENS // QGK_BM, QGK_WIDTH // PROJ_BN)
    return pl.pallas_call(
        _qgk_kernel,
        grid=grid,
        in_specs=[
            pl.BlockSpec((QGK_BM, D_MODEL), lambda i, j: (i, 0)),
            pl.BlockSpec((QGK_BM, LANES), lambda i, j: (i, 0)),
            pl.BlockSpec((QGK_BM, LANES), lambda i, j: (i, 0)),
            pl.BlockSpec((D_MODEL, PROJ_BN), lambda i, j: (0, j)),
            pl.BlockSpec((1, PROJ_BN), lambda i, j: (0, j)),
            pl.BlockSpec((2, SUBLANES, ROPE_PERIOD), lambda i, j: (0, 0, 0)),
        ],
        out_specs=pl.BlockSpec((QGK_BM, PROJ_BN), lambda i, j: (i, j)),
        out_shape=jax.ShapeDtypeStruct((TOKENS, QGK_WIDTH), BF16),
        scratch_shapes=[
            pltpu.VMEM((QGK_BM, D_MODEL), BF16),
            pltpu.VMEM((2, 3, QGK_BM, ROPE_PERIOD), F32),
        ],
        compiler_params=pltpu.CompilerParams(
            dimension_semantics=("arbitrary", "arbitrary"),
            vmem_limit_bytes=VMEM_LIMIT_BYTES),
        name="qgk_proj",
    )(x1, cos_t, sin_t, w, bias, masks)


PAIRS_PER_KV = Q_PER_KV // 2
STACK_ROWS = PAIRS_PER_KV * WINDOW
KV_GROUP_Q = Q_PER_KV * ATT_HEAD_DIM
SCORE_SCALE_LOG2 = ATT_HEAD_DIM ** -0.5 * LOG2E


def _attn_kernel(sink_ref, mask_ref, q_ref, gate_ref, kvp_ref, kvc_ref, o_ref):
    mask_bias = jnp.tile(mask_ref[0], (PAIRS_PER_KV, 1))
    lane = lax.broadcasted_iota(jnp.int32, (STACK_ROWS, LANES), 1)
    low_half = lane < ATT_HEAD_DIM
    kv_lane = lax.broadcasted_iota(jnp.int32, (2 * WINDOW, LANES), 1)
    kv_low = kv_lane < ATT_HEAD_DIM
    pair_of_row = lax.broadcasted_iota(jnp.int32, (STACK_ROWS, 1), 0) // WINDOW

    values, probs, sink_terms = [], [], []
    for hk in range(KV_HEADS):
        ks = slice(hk * KV_PACK, (hk + 1) * KV_PACK)
        kv = jnp.concatenate([kvp_ref[:, ks], kvc_ref[:, ks]], axis=0)
        kv_kv = kv[:, 0:LANES]
        kv_vk = kv[:, LANES:2 * LANES]
        values.append((jnp.where(kv_low, kv_vk, jnp.ones_like(kv_vk)),
                       jnp.where(kv_low, jnp.ones_like(kv_kv), kv_kv)))
        qs = jnp.concatenate(
            [q_ref[:, hk * KV_GROUP_Q + p * LANES:hk * KV_GROUP_Q + (p + 1) * LANES]
             for p in range(PAIRS_PER_KV)], axis=0)
        for half in range(2):
            qm = jnp.where(low_half if half == 0 else ~low_half, qs, jnp.zeros_like(qs))
            keys = kv_kv if half == 0 else kv_vk
            s = lax.dot_general(qm, keys, (((1,), (1,)), ((), ())), preferred_element_type=F32)
            s = s * SCORE_SCALE_LOG2 + mask_bias
            sk = jnp.zeros((STACK_ROWS, 1), F32)
            for p in range(PAIRS_PER_KV):
                sk = jnp.where(pair_of_row == p, sink_ref[hk * Q_PER_KV + 2 * p + half] * LOG2E, sk)
            m = jnp.maximum(jnp.max(s, axis=-1, keepdims=True), sk)
            probs.append(jnp.exp2(s - m).astype(BF16))
            sink_terms.append(jnp.exp2(sk - m))

    accs = [jnp.dot(probs[2 * hk + half], values[hk][half], preferred_element_type=F32)
            for hk in range(KV_HEADS) for half in range(2)]

    for hk in range(KV_HEADS):
        acc_lo, acc_hi = accs[2 * hk], accs[2 * hk + 1]
        num = jnp.where(low_half, acc_lo, acc_hi)
        den = (pltpu.roll(jnp.where(low_half, acc_hi, acc_lo), ATT_HEAD_DIM, axis=1)
               + jnp.where(low_half, sink_terms[2 * hk], sink_terms[2 * hk + 1]))
        o_pairs = num / den
        for p in range(PAIRS_PER_KV):
            cols = slice(hk * KV_GROUP_Q + p * LANES, hk * KV_GROUP_Q + (p + 1) * LANES)
            o_ref[:, cols] = (o_pairs[p * WINDOW:(p + 1) * WINDOW]
                              * gate_ref[:, cols].astype(F32)).astype(BF16)


def _attention(qgk, sinks, mask_bias):
    nb = SEQ // WINDOW
    row = lambda b, c: b * nb + c
    kv_col = (2 * Q_WIDTH) // (KV_HEADS * KV_PACK)
    return pl.pallas_call(
        _attn_kernel,
        grid=(BATCH, nb),
        in_specs=[
            pl.BlockSpec(memory_space=pltpu.SMEM),
            pl.BlockSpec((1, WINDOW, 2 * WINDOW), lambda b, c: (jnp.minimum(c, 1), 0, 0)),
            pl.BlockSpec((WINDOW, Q_WIDTH), lambda b, c: (row(b, c), 0)),
            pl.BlockSpec((WINDOW, Q_WIDTH), lambda b, c: (row(b, c), 1)),
            pl.BlockSpec((WINDOW, KV_HEADS * KV_PACK), lambda b, c: (row(b, jnp.maximum(c - 1, 0)), kv_col)),
            pl.BlockSpec((WINDOW, KV_HEADS * KV_PACK), lambda b, c: (row(b, c), kv_col)),
        ],
        out_specs=pl.BlockSpec((WINDOW, Q_WIDTH), lambda b, c: (row(b, c), 0)),
        out_shape=jax.ShapeDtypeStruct((TOKENS, Q_WIDTH), BF16),
        compiler_params=pltpu.CompilerParams(
            dimension_semantics=("arbitrary", "arbitrary"),
            vmem_limit_bytes=VMEM_LIMIT_BYTES),
        name="swa_attention",
    )(sinks, mask_bias, qgk, qgk, qgk, qgk)


def _expand_matrix():
    e = np.zeros((2 * LANES, D_INNER), np.float32)
    for h in range(SSM_HEADS):
        e[h, h * SSM_HEAD_DIM:(h + 1) * SSM_HEAD_DIM] = 1.0
        e[LANES + h, h * SSM_HEAD_DIM:(h + 1) * SSM_HEAD_DIM] = 1.0
    return jnp.asarray(e, BF16)


def _rope_masks():
    m = np.zeros((2, SUBLANES, ROPE_PERIOD), np.float32)
    d = np.arange(ATT_HEAD_DIM)
    sect = np.stack([1.0 * (d < ROT_DIM), -1.0 * (d < ROPE_SHIFT),
                     1.0 * ((d >= ROPE_SHIFT) & (d < ROT_DIM))]).astype(np.float32)
    for s in range(ROPE_PERIOD // ATT_HEAD_DIM):
        m[0, 0:3, s * ATT_HEAD_DIM:(s + 1) * ATT_HEAD_DIM] = sect
        if s in (0, 3):
            m[1, 0:3, s * ATT_HEAD_DIM:(s + 1) * ATT_HEAD_DIM] = sect
    return jnp.asarray(m)


def _window_mask_bias():
    qi = np.arange(WINDOW)[:, None]
    col = np.arange(2 * WINDOW)[None, :]
    band = (col > qi) & (col <= qi + WINDOW)
    first = band & (col >= WINDOW)
    return jnp.asarray(np.where(np.stack([first, band]), 0.0, -np.inf).astype(np.float32))


def _pack_kv(a):
    k = a[..., :KV_WIDTH].reshape(a.shape[:-1] + (KV_HEADS, ATT_HEAD_DIM))
    v = a[..., KV_WIDTH:].reshape(a.shape[:-1] + (KV_HEADS, ATT_HEAD_DIM))
    return jnp.concatenate([k, v, v, k], axis=-1).reshape(a.shape[:-1] + (KV_HEADS * KV_PACK,))


def kernel(x, positions, ln_g, ln_b, a_w_in, a_conv_w, a_conv_b, a_dt_bias, a_log, a_d, a_norm_w, a_w_out, kv_w, kv_b, b_w_in, b_q_bias, b_sinks, b_w_out):
    x2 = x.reshape(TOKENS, D_MODEL)

    w_in = a_w_in[0].astype(BF16)
    w_dt = jnp.pad(w_in[:, ZX_WIDTH:], ((0, 0), (0, LANES - SSM_HEADS)))
    zx, dt_raw = _in_proj(x2, w_in, w_dt)

    pad_heads = lambda a: jnp.pad(a, (0, LANES - SSM_HEADS)).reshape(1, LANES)
    y = _ssd(zx, dt_raw, a_conv_w[0], a_conv_b[0].reshape(1, CONV_DIM),
             pad_heads(a_dt_bias[0]), pad_heads(a_log[0]),
             jnp.repeat(a_d[0], SSM_HEAD_DIM).reshape(1, D_INNER),
             a_norm_w[0].reshape(1, D_INNER), _expand_matrix())
    x1 = _out_ln(y, a_w_out[0].astype(BF16), x2, ln_g[0].reshape(1, D_MODEL),
                 ln_b[0].reshape(1, D_MODEL), 256, "ssd_out_ln")

    w_qgk = jnp.concatenate([b_w_in[0].astype(BF16), _pack_kv(kv_w.astype(BF16))], axis=1)
    bias = jnp.concatenate([b_q_bias[0], jnp.zeros((Q_WIDTH,), F32), _pack_kv(kv_b)]).reshape(1, QGK_WIDTH)
    inv_freq = ROPE_THETA ** (-jnp.arange(0, ROT_DIM, 2, dtype=F32) / ROT_DIM)
    invf = jnp.tile(inv_freq, LANES // ROPE_SHIFT).reshape(1, LANES)
    tokens_per_row = LANES // ROPE_SHIFT
    pos_rep = jnp.repeat(positions.reshape(TABLE_ROWS, tokens_per_row), ROPE_SHIFT, axis=1)
    cos_c, sin_c = _rope_tables(pos_rep, invf)
    cos_t = jnp.tile(cos_c.reshape(TOKENS, ROPE_SHIFT), (1, tokens_per_row))
    sin_t = jnp.tile(sin_c.reshape(TOKENS, ROPE_SHIFT), (1, tokens_per_row))
    qgk = _qgk_proj(x1, cos_t, sin_t, w_qgk, bias, _rope_masks())
    o = _attention(qgk, b_sinks[0], _window_mask_bias())
    out = _out_ln(o, b_w_out[0].astype(BF16), x1, ln_g[1].reshape(1, D_MODEL),
                  ln_b[1].reshape(1, D_MODEL), 512, "swa_out_ln")
    return out.reshape(BATCH, SEQ, D_MODEL)
```

```python
import math

import numpy as np
import jax
import jax.numpy as jnp
from jax import lax
from jax.experimental import pallas as pl
from jax.experimental.pallas import tpu as pltpu

D_MODEL = 2048
BATCH = 8
SEQ = 4096
TOKENS = BATCH * SEQ
DEPTH = 2

D_INNER = 4096
SSM_HEAD_DIM = 64
SSM_HEADS = 64
SSM_GROUPS = 8
HEADS_PER_GROUP = SSM_HEADS // SSM_GROUPS
GROUP_WIDTH = D_INNER // SSM_GROUPS
D_STATE = 128
D_CONV = 4
CHUNK = 128
GN = SSM_GROUPS * D_STATE
CONV_DIM = D_INNER + 2 * GN
ZX_WIDTH = D_INNER + CONV_DIM

ATT_HEAD_DIM = 64
ATT_HEADS = 32
KV_HEADS = 4
Q_PER_KV = 8
Q_WIDTH = 2048
KV_WIDTH = 256
WINDOW = 128
ROT_DIM = 16
ROPE_SHIFT = ROT_DIM // 2
ROPE_THETA = 500000.0
KV_PACK = 4 * ATT_HEAD_DIM
QGK_WIDTH = 2 * Q_WIDTH + KV_HEADS * KV_PACK

ALPHA = (2.0 * DEPTH) ** 0.25
EPS = 1e-5
LOG2E = math.log2(math.e)

LANES = 128
SUBLANES = 8
BF16_SUBLANES = 16
VMEM_LIMIT_BYTES = 56 * 1024 * 1024

F32 = jnp.float32
BF16 = jnp.bfloat16


def _silu(v):
    h = 0.5 * v
    return h + h * jnp.tanh(h)


def _split_bf16(v, terms):
    parts = []
    rem = v
    for _ in range(terms):
        p = rem.astype(BF16)
        parts.append(p)
        rem = rem - p.astype(F32)
    return parts


PROJ_BM = 1024
PROJ_BN = 1024
PROJ_SUB = 256
PROJ_TILES = ZX_WIDTH // PROJ_BN
Z_TILES = D_INNER // PROJ_BN


def _in_proj_kernel(x_ref, w_ref, wdt_ref, zx_ref, dt_ref, xb_ref):
    j = pl.program_id(1)

    def gate_tile(first):
        for r in range(PROJ_BM // PROJ_SUB):
            rows = slice(r * PROJ_SUB, (r + 1) * PROJ_SUB)
            if first:
                xb_ref[rows, :] = x_ref[rows, :].astype(BF16)
                dt_ref[rows, :] = jnp.dot(xb_ref[rows, :], wdt_ref[...], preferred_element_type=F32)
            acc = jnp.dot(xb_ref[rows, :], w_ref[...], preferred_element_type=F32)
            zx_ref[rows, :] = _silu(acc).astype(BF16)

    @pl.when(j == 0)
    def _():
        gate_tile(True)

    @pl.when((j > 0) & (j < Z_TILES))
    def _():
        gate_tile(False)

    @pl.when(j >= Z_TILES)
    def _():
        zx_ref[...] = jnp.dot(xb_ref[...], w_ref[...], preferred_element_type=F32).astype(BF16)


def _in_proj(x2, w_in, w_dt):
    grid = (TOKENS // PROJ_BM, PROJ_TILES)
    return pl.pallas_call(
        _in_proj_kernel,
        grid=grid,
        in_specs=[
            pl.BlockSpec((PROJ_BM, D_MODEL), lambda i, j: (i, 0)),
            pl.BlockSpec((D_MODEL, PROJ_BN), lambda i, j: (0, j)),
            pl.BlockSpec((D_MODEL, LANES), lambda i, j: (0, 0)),
        ],
        out_specs=[
            pl.BlockSpec((PROJ_BM, PROJ_BN), lambda i, j: (i, j)),
            pl.BlockSpec((PROJ_BM, LANES), lambda i, j: (i, 0)),
        ],
        out_shape=[
            jax.ShapeDtypeStruct((TOKENS, ZX_WIDTH), BF16),
            jax.ShapeDtypeStruct((TOKENS, LANES), F32),
        ],
        scratch_shapes=[pltpu.VMEM((PROJ_BM, D_MODEL), BF16)],
        compiler_params=pltpu.CompilerParams(
            dimension_semantics=("arbitrary", "arbitrary"),
            vmem_limit_bytes=VMEM_LIMIT_BYTES),
        name="in_proj",
    )(x2, w_in, w_dt)


XS_HALF = D_INNER // 2
EXPAND_ROWS = 2 * CHUNK + BF16_SUBLANES
CONV_SLABS = CONV_DIM // LANES
CONV_STRIDE = 4


def _ssd_kernel(z_ref, xa_ref, xb_ref, bc_ref, dt_ref, cw_ref, cb_ref, dtb_ref, alog_ref,
                dsk_ref, nw_ref, e2_ref, y_ref, state_ref, stage_ref, conv_ref):
    @pl.when(pl.program_id(1) == 0)
    def _():
        state_ref[...] = jnp.zeros_like(state_ref)
        for s in range(CONV_SLABS):
            stage_ref[s, 0:SUBLANES, :] = jnp.zeros((SUBLANES, LANES), F32)

    for k3, src in enumerate((xa_ref, xb_ref, bc_ref)):
        for s in range(XS_HALF // LANES):
            slab = k3 * (XS_HALF // LANES) + s
            stage_ref[slab, SUBLANES:SUBLANES + CHUNK, :] = src[:, s * LANES:(s + 1) * LANES].astype(F32)
    per_residue = CHUNK // CONV_STRIDE
    for slab in range(CONV_SLABS):
        cols = slice(slab * LANES, (slab + 1) * LANES)
        taps = [cw_ref[D_CONV - 1 - j:D_CONV - j, cols] for j in range(D_CONV)]
        bias = cb_ref[:, cols]
        for k in range(CONV_STRIDE):
            acc = bias
            for j in range(D_CONV):
                acc = acc + stage_ref[slab, pl.ds(SUBLANES + k - j, per_residue, stride=CONV_STRIDE), :] * taps[j]
            conv_ref[slab, pl.ds(k, per_residue, stride=CONV_STRIDE), :] = _silu(acc)
        stage_ref[slab, 0:SUBLANES, :] = stage_ref[slab, CHUNK:CHUNK + SUBLANES, :]

    v = dt_ref[...] + dtb_ref[...]
    dt = jnp.maximum(v, 0.0) + jnp.log1p(jnp.exp(-jnp.abs(v)))
    da = dt * (-jnp.exp(alog_ref[...]))
    rq = lax.broadcasted_iota(jnp.int32, (CHUNK, CHUNK), 0)
    cs = lax.broadcasted_iota(jnp.int32, (CHUNK, CHUNK), 1)
    tri = rq >= cs
    cum3 = jnp.dot(jnp.where(tri, 1.0, 0.0).astype(BF16),
                   jnp.concatenate(_split_bf16(da, 3), axis=1), preferred_element_type=F32)
    cum = cum3[:, 0:LANES] + cum3[:, LANES:2 * LANES] + cum3[:, 2 * LANES:3 * LANES]
    c2 = cum * LOG2E
    c2_src_t = (c2 - jnp.log2(dt)).T
    last = cum[CHUNK - 1:CHUNK, :]
    stack = jnp.concatenate(
        [jnp.exp(cum), dt * jnp.exp(last - cum),
         jnp.broadcast_to(jnp.exp(last), (BF16_SUBLANES, LANES))], axis=0)
    expand_lhs = jnp.concatenate(_split_bf16(stack, 2), axis=1)

    lane = lax.broadcasted_iota(jnp.int32, (CHUNK, LANES), 1)
    keep_lo = jnp.where(lane < SSM_HEAD_DIM, 1.0, 0.0).astype(BF16)
    keep_hi = jnp.where(lane >= SSM_HEAD_DIM, 1.0, 0.0).astype(BF16)

    for g in range(SSM_GROUPS):
        gs = slice(g * GROUP_WIDTH, (g + 1) * GROUP_WIDTH)
        ex = jnp.dot(expand_lhs, e2_ref[:, gs], preferred_element_type=F32)
        ecum_x = ex[0:CHUNK]
        dstate_x = ex[CHUNK:2 * CHUNK]
        elast_x = ex[2 * CHUNK:2 * CHUNK + 1]

        slabs_per_group = GROUP_WIDTH // LANES
        xs = jnp.concatenate([conv_ref[g * slabs_per_group + q] for q in range(slabs_per_group)], axis=1)
        xsb = xs.astype(BF16)
        bm = conv_ref[D_INNER // LANES + g]
        bb = bm.astype(BF16)
        cb = conv_ref[(D_INNER + GN) // LANES + g].astype(BF16)
        cbt = lax.dot_general(cb, bb, (((1,), (1,)), ((), ())),
                              preferred_element_type=F32).astype(BF16)

        s_prev = state_ref[:, gs]
        y_off = jnp.dot(cb, s_prev.astype(BF16), preferred_element_type=F32) * ecum_x
        state_ref[:, gs] = s_prev * elast_x + jnp.dot(
            bm.T.astype(BF16), (xs * dstate_x).astype(BF16), preferred_element_type=F32)

        pairs = []
        for p in range(HEADS_PER_GROUP // 2):
            xp = xsb[:, p * LANES:(p + 1) * LANES]
            rhs = jnp.concatenate([xp * keep_lo, xp * keep_hi], axis=0)
            ms = []
            for half in range(2):
                h = g * HEADS_PER_GROUP + 2 * p + half
                decay = jnp.where(tri, jnp.exp2(c2[:, h:h + 1] - c2_src_t[h:h + 1, :]), 0.0)
                ms.append(cbt * decay.astype(BF16))
            pairs.append(jnp.dot(jnp.concatenate(ms, axis=1), rhs, preferred_element_type=F32))
        y = jnp.concatenate(pairs, axis=1) + y_off + xs * dsk_ref[:, gs]
        y = y * z_ref[:, gs].astype(F32)
        ms = jnp.mean(y * y, axis=-1, keepdims=True)
        y_ref[:, gs] = (y * lax.rsqrt(ms + EPS) * nw_ref[:, gs]).astype(BF16)


def _ssd(zx, dt_raw, conv_w, conv_b, dt_bias, a_log, d_skip_x, norm_w, e2):
    nc = SEQ // CHUNK
    row = lambda b, c: b * nc + c
    const = lambda b, c: (0, 0)
    zx_cols = ZX_WIDTH // XS_HALF
    assert zx_cols == 5
    return pl.pallas_call(
        _ssd_kernel,
        grid=(BATCH, nc),
        in_specs=[
            pl.BlockSpec((CHUNK, D_INNER), lambda b, c: (row(b, c), 0)),
            pl.BlockSpec((CHUNK, XS_HALF), lambda b, c: (row(b, c), 2)),
            pl.BlockSpec((CHUNK, XS_HALF), lambda b, c: (row(b, c), 3)),
            pl.BlockSpec((CHUNK, XS_HALF), lambda b, c: (row(b, c), 4)),
            pl.BlockSpec((CHUNK, LANES), lambda b, c: (row(b, c), 0)),
            pl.BlockSpec((D_CONV, CONV_DIM), const),
            pl.BlockSpec((1, CONV_DIM), const),
            pl.BlockSpec((1, LANES), const),
            pl.BlockSpec((1, LANES), const),
            pl.BlockSpec((1, D_INNER), const),
            pl.BlockSpec((1, D_INNER), const),
            pl.BlockSpec((2 * LANES, D_INNER), const),
        ],
        out_specs=pl.BlockSpec((CHUNK, D_INNER), lambda b, c: (row(b, c), 0)),
        out_shape=jax.ShapeDtypeStruct((TOKENS, D_INNER), BF16),
        scratch_shapes=[
            pltpu.VMEM((D_STATE, D_INNER), F32),
            pltpu.VMEM((CONV_SLABS, SUBLANES + CHUNK, LANES), F32),
            pltpu.VMEM((CONV_SLABS, CHUNK, LANES), F32),
        ],
        compiler_params=pltpu.CompilerParams(
            dimension_semantics=("arbitrary", "arbitrary"),
            vmem_limit_bytes=VMEM_LIMIT_BYTES),
        name="ssd",
    )(zx, zx, zx, zx, dt_raw, conv_w, conv_b, dt_bias, a_log, d_skip_x, norm_w, e2)


def _out_ln_kernel(y_ref, w_ref, x_ref, g_ref, b_ref, o_ref):
    for r in range(y_ref.shape[0] // PROJ_SUB):
        rows = slice(r * PROJ_SUB, (r + 1) * PROJ_SUB)
        h = jnp.dot(y_ref[rows, :], w_ref[...], preferred_element_type=F32)
        v = ALPHA * x_ref[rows, :] + h
        mu = jnp.mean(v, axis=-1, keepdims=True)
        d = v - mu
        var = jnp.mean(d * d, axis=-1, keepdims=True)
        o_ref[rows, :] = d * lax.rsqrt(var + EPS) * g_ref[...] + b_ref[...]


def _out_ln(y, w, x2, g, b, bm, name):
    k = y.shape[1]
    return pl.pallas_call(
        _out_ln_kernel,
        grid=(TOKENS // bm,),
        in_specs=[
            pl.BlockSpec((bm, k), lambda i: (i, 0)),
            pl.BlockSpec((k, D_MODEL), lambda i: (0, 0), pipeline_mode=pl.Buffered(1)),
            pl.BlockSpec((bm, D_MODEL), lambda i: (i, 0)),
            pl.BlockSpec((1, D_MODEL), lambda i: (0, 0)),
            pl.BlockSpec((1, D_MODEL), lambda i: (0, 0)),
        ],
        out_specs=pl.BlockSpec((bm, D_MODEL), lambda i: (i, 0)),
        out_shape=jax.ShapeDtypeStruct((TOKENS, D_MODEL), F32),
        compiler_params=pltpu.CompilerParams(
            dimension_semantics=("arbitrary",),
            vmem_limit_bytes=VMEM_LIMIT_BYTES),
        name=name,
    )(y, w, x2, g, b)


TOKENS_PER_ROW = LANES // ROPE_SHIFT
COMPACT_ROWS = SEQ // TOKENS_PER_ROW
SPLIT_TERMS = 3


def _rope_table_kernel(pos_ref, invf_ref, sel_ref, cos_ref, sin_ref):
    ang = pos_ref[...].astype(F32) * invf_ref[...]
    for table, out_ref in ((jnp.cos(ang), cos_ref), (jnp.sin(ang), sin_ref)):
        lhs = jnp.concatenate(_split_bf16(table, SPLIT_TERMS), axis=1)
        for u in range(TOKENS_PER_ROW):
            out_ref[pl.ds(u, COMPACT_ROWS, stride=TOKENS_PER_ROW), :] = jnp.dot(
                lhs, sel_ref[u], preferred_element_type=F32)


def _rope_tables(pos_rep, invf, sel):
    return pl.pallas_call(
        _rope_table_kernel,
        grid=(BATCH,),
        in_specs=[
            pl.BlockSpec((COMPACT_ROWS, LANES), lambda b: (b, 0)),
            pl.BlockSpec((1, LANES), lambda b: (0, 0)),
            pl.BlockSpec((TOKENS_PER_ROW, SPLIT_TERMS * LANES, LANES), lambda b: (0, 0, 0)),
        ],
        out_specs=[pl.BlockSpec((SEQ, LANES), lambda b: (b, 0))] * 2,
        out_shape=[jax.ShapeDtypeStruct((TOKENS, LANES), F32)] * 2,
        compiler_params=pltpu.CompilerParams(
            dimension_semantics=("arbitrary",),
            vmem_limit_bytes=VMEM_LIMIT_BYTES),
        name="rope_tables",
    )(pos_rep, invf, sel)


QGK_BM = 1024
ROPE_PERIOD = KV_PACK
Q_TILES = Q_WIDTH // PROJ_BN


def _qgk_kernel(x_ref, cos_ref, sin_ref, w_ref, bias_ref, mask_ref, o_ref, xb_ref, tab_ref):
    j = pl.program_id(1)

    def rope_tile(first):
        t = 0 if first else jnp.where(j >= 2 * Q_TILES, 1, 0)
        reps = PROJ_BN // ROPE_PERIOD
        for r in range(QGK_BM // PROJ_SUB):
            rows = slice(r * PROJ_SUB, (r + 1) * PROJ_SUB)
            if first:
                xb_ref[rows, :] = x_ref[rows, :].astype(BF16)
                cos_p = jnp.tile(cos_ref[rows, :], (1, ROPE_PERIOD // LANES))
                sin_p = jnp.tile(sin_ref[rows, :], (1, ROPE_PERIOD // LANES))
                for kind in range(2):
                    m = mask_ref[kind]
                    tab_ref[kind, 0, rows, :] = 1.0 + m[0:1] * (cos_p - 1.0)
                    tab_ref[kind, 1, rows, :] = m[1:2] * sin_p
                    tab_ref[kind, 2, rows, :] = m[2:3] * sin_p
            acc = jnp.dot(xb_ref[rows, :], w_ref[...], preferred_element_type=F32) + bias_ref[...]
            upper = pltpu.roll(acc, PROJ_BN - ROPE_SHIFT, axis=1)
            lower = pltpu.roll(acc, ROPE_SHIFT, axis=1)
            out = (acc * jnp.tile(tab_ref[t, 0, rows, :], (1, reps))
                   + upper * jnp.tile(tab_ref[t, 1, rows, :], (1, reps))
                   + lower * jnp.tile(tab_ref[t, 2, rows, :], (1, reps)))
            o_ref[rows, :] = out.astype(BF16)

    has_rope = (j < Q_TILES) | (j >= 2 * Q_TILES)

    @pl.when(j == 0)
    def _():
        rope_tile(True)

    @pl.when(has_rope & (j > 0))
    def _():
        rope_tile(False)

    @pl.when(jnp.logical_not(has_rope))
    def _():
        for r in range(QGK_BM // PROJ_SUB):
            rows = slice(r * PROJ_SUB, (r + 1) * PROJ_SUB)
            acc = jnp.dot(xb_ref[rows, :], w_ref[...], preferred_element_type=F32) + bias_ref[...]
            o_ref[rows, :] = _silu(acc).astype(BF16)


def _qgk_proj(x1, cos_t, sin_t, w, bias, masks):
    grid = (TOKENS // QGK_BM, QGK_WIDTH // PROJ_BN)
    return pl.pallas_call(
        _qgk_kernel,
        grid=grid,
        in_specs=[
            pl.BlockSpec((QGK_BM, D_MODEL), lambda i, j: (i, 0)),
            pl.BlockSpec((QGK_BM, LANES), lambda i, j: (i, 0)),
            pl.BlockSpec((QGK_BM, LANES), lambda i, j: (i, 0)),
            pl.BlockSpec((D_MODEL, PROJ_BN), lambda i, j: (0, j)),
            pl.BlockSpec((1, PROJ_BN), lambda i, j: (0, j)),
            pl.BlockSpec((2, SUBLANES, ROPE_PERIOD), lambda i, j: (0, 0, 0)),
        ],
        out_specs=pl.BlockSpec((QGK_BM, PROJ_BN), lambda i, j: (i, j)),
        out_shape=jax.ShapeDtypeStruct((TOKENS, QGK_WIDTH), BF16),
        scratch_shapes=[
            pltpu.VMEM((QGK_BM, D_MODEL), BF16),
            pltpu.VMEM((2, 3, QGK_BM, ROPE_PERIOD), F32),
        ],
        compiler_params=pltpu.CompilerParams(
            dimension_semantics=("arbitrary", "arbitrary"),
            vmem_limit_bytes=VMEM_LIMIT_BYTES),
        name="qgk_proj",
    )(x1, cos_t, sin_t, w, bias, masks)


PAIRS_PER_KV = Q_PER_KV // 2
STACK_ROWS = PAIRS_PER_KV * WINDOW
KV_GROUP_Q = Q_PER_KV * ATT_HEAD_DIM
SCORE_SCALE_LOG2 = ATT_HEAD_DIM ** -0.5 * LOG2E


def _attn_kernel(sink_ref, mask_ref, q_ref, gate_ref, kvp_ref, kvc_ref, o_ref):
    mask_bias = jnp.tile(mask_ref[0], (PAIRS_PER_KV, 1))
    lane = lax.broadcasted_iota(jnp.int32, (STACK_ROWS, LANES), 1)
    low_half = lane < ATT_HEAD_DIM
    kv_lane = lax.broadcasted_iota(jnp.int32, (2 * WINDOW, LANES), 1)
    kv_low = kv_lane < ATT_HEAD_DIM
    pair_of_row = lax.broadcasted_iota(jnp.int32, (STACK_ROWS, 1), 0) // WINDOW

    values, probs, sink_terms = [], [], []
    for hk in range(KV_HEADS):
        ks = slice(hk * KV_PACK, (hk + 1) * KV_PACK)
        kv = jnp.concatenate([kvp_ref[:, ks], kvc_ref[:, ks]], axis=0)
        kv_kv = kv[:, 0:LANES]
        kv_vk = kv[:, LANES:2 * LANES]
        values.append((jnp.where(kv_low, kv_vk, jnp.ones_like(kv_vk)),
                       jnp.where(kv_low, jnp.ones_like(kv_kv), kv_kv)))
        qs = jnp.concatenate(
            [q_ref[:, hk * KV_GROUP_Q + p * LANES:hk * KV_GROUP_Q + (p + 1) * LANES]
             for p in range(PAIRS_PER_KV)], axis=0)
        for half in range(2):
            qm = jnp.where(low_half if half == 0 else ~low_half, qs, jnp.zeros_like(qs))
            keys = kv_kv if half == 0 else kv_vk
            s = lax.dot_general(qm, keys, (((1,), (1,)), ((), ())), preferred_element_type=F32)
            s = s * SCORE_SCALE_LOG2 + mask_bias
            sk = jnp.zeros((STACK_ROWS, 1), F32)
            for p in range(PAIRS_PER_KV):
                sk = jnp.where(pair_of_row == p, sink_ref[hk * Q_PER_KV + 2 * p + half] * LOG2E, sk)
            m = jnp.maximum(jnp.max(s, axis=-1, keepdims=True), sk)
            probs.append(jnp.exp2(s - m).astype(BF16))
            sink_terms.append(jnp.exp2(sk - m))

    accs = [jnp.dot(probs[2 * hk + half], values[hk][half], preferred_element_type=F32)
            for hk in range(KV_HEADS) for half in range(2)]

    for hk in range(KV_HEADS):
        acc_lo, acc_hi = accs[2 * hk], accs[2 * hk + 1]
        num = jnp.where(low_half, acc_lo, acc_hi)
        den = (pltpu.roll(jnp.where(low_half, acc_hi, acc_lo), ATT_HEAD_DIM, axis=1)
               + jnp.where(low_half, sink_terms[2 * hk], sink_terms[2 * hk + 1]))
        o_pairs = num / den
        for p in range(PAIRS_PER_KV):
            cols = slice(hk * KV_GROUP_Q + p * LANES, hk * KV_GROUP_Q + (p + 1) * LANES)
            o_ref[:, cols] = (o_pairs[p * WINDOW:(p + 1) * WINDOW]
                              * gate_ref[:, cols].astype(F32)).astype(BF16)


def _attention(qgk, sinks, mask_bias):
    nb = SEQ // WINDOW
    row = lambda b, c: b * nb + c
    kv_col = (2 * Q_WIDTH) // (KV_HEADS * KV_PACK)
    return pl.pallas_call(
        _attn_kernel,
        grid=(BATCH, nb),
        in_specs=[
            pl.BlockSpec(memory_space=pltpu.SMEM),
            pl.BlockSpec((1, WINDOW, 2 * WINDOW), lambda b, c: (jnp.minimum(c, 1), 0, 0)),
            pl.BlockSpec((WINDOW, Q_WIDTH), lambda b, c: (row(b, c), 0)),
            pl.BlockSpec((WINDOW, Q_WIDTH), lambda b, c: (row(b, c), 1)),
            pl.BlockSpec((WINDOW, KV_HEADS * KV_PACK), lambda b, c: (row(b, jnp.maximum(c - 1, 0)), kv_col)),
            pl.BlockSpec((WINDOW, KV_HEADS * KV_PACK), lambda b, c: (row(b, c), kv_col)),
        ],
        out_specs=pl.BlockSpec((WINDOW, Q_WIDTH), lambda b, c: (row(b, c), 0)),
        out_shape=jax.ShapeDtypeStruct((TOKENS, Q_WIDTH), BF16),
        compiler_params=pltpu.CompilerParams(
            dimension_semantics=("arbitrary", "arbitrary"),
            vmem_limit_bytes=VMEM_LIMIT_BYTES),
        name="swa_attention",
    )(sinks, mask_bias, qgk, qgk, qgk, qgk)


def _expand_matrix():
    e = np.zeros((2 * LANES, D_INNER), np.float32)
    for h in range(SSM_HEADS):
        e[h, h * SSM_HEAD_DIM:(h + 1) * SSM_HEAD_DIM] = 1.0
        e[LANES + h, h * SSM_HEAD_DIM:(h + 1) * SSM_HEAD_DIM] = 1.0
    return jnp.asarray(e, BF16)


def _rope_masks():
    m = np.zeros((2, SUBLANES, ROPE_PERIOD), np.float32)
    d = np.arange(ATT_HEAD_DIM)
    sect = np.stack([1.0 * (d < ROT_DIM), -1.0 * (d < ROPE_SHIFT),
                     1.0 * ((d >= ROPE_SHIFT) & (d < ROT_DIM))]).astype(np.float32)
    for s in range(ROPE_PERIOD // ATT_HEAD_DIM):
        m[0, 0:3, s * ATT_HEAD_DIM:(s + 1) * ATT_HEAD_DIM] = sect
        if s in (0, 3):
            m[1, 0:3, s * ATT_HEAD_DIM:(s + 1) * ATT_HEAD_DIM] = sect
    return jnp.asarray(m)


def _rope_select():
    sel = np.zeros((TOKENS_PER_ROW, SPLIT_TERMS * LANES, LANES), np.float32)
    lane = np.arange(LANES)
    for u in range(TOKENS_PER_ROW):
        for t in range(SPLIT_TERMS):
            sel[u, t * LANES + u * ROPE_SHIFT + lane % ROPE_SHIFT, lane] = 1.0
    return jnp.asarray(sel, BF16)


def _window_mask_bias():
    qi = np.arange(WINDOW)[:, None]
    col = np.arange(2 * WINDOW)[None, :]
    band = (col > qi) & (col <= qi + WINDOW)
    first = band & (col >= WINDOW)
    return jnp.asarray(np.where(np.stack([first, band]), 0.0, -np.inf).astype(np.float32))


def _pack_kv(a):
    k = a[..., :KV_WIDTH].reshape(a.shape[:-1] + (KV_HEADS, ATT_HEAD_DIM))
    v = a[..., KV_WIDTH:].reshape(a.shape[:-1] + (KV_HEADS, ATT_HEAD_DIM))
    return jnp.concatenate([k, v, v, k], axis=-1).reshape(a.shape[:-1] + (KV_HEADS * KV_PACK,))


def kernel(x, positions, ln_g, ln_b, a_w_in, a_conv_w, a_conv_b, a_dt_bias, a_log, a_d, a_norm_w, a_w_out, kv_w, kv_b, b_w_in, b_q_bias, b_sinks, b_w_out):
    x2 = x.reshape(TOKENS, D_MODEL)

    w_in = a_w_in[0].astype(BF16)
    w_dt = jnp.pad(w_in[:, ZX_WIDTH:], ((0, 0), (0, LANES - SSM_HEADS)))
    zx, dt_raw = _in_proj(x2, w_in, w_dt)

    pad_heads = lambda a: jnp.pad(a, (0, LANES - SSM_HEADS)).reshape(1, LANES)
    y = _ssd(zx, dt_raw, a_conv_w[0], a_conv_b[0].reshape(1, CONV_DIM),
             pad_heads(a_dt_bias[0]), pad_heads(a_log[0]),
             jnp.repeat(a_d[0], SSM_HEAD_DIM).reshape(1, D_INNER),
             a_norm_w[0].reshape(1, D_INNER), _expand_matrix())
    x1 = _out_ln(y, a_w_out[0].astype(BF16), x2, ln_g[0].reshape(1, D_MODEL),
                 ln_b[0].reshape(1, D_MODEL), 512, "ssd_out_ln")

    w_qgk = jnp.concatenate([b_w_in[0].astype(BF16), _pack_kv(kv_w.astype(BF16))], axis=1)
    bias = jnp.concatenate([b_q_bias[0], jnp.zeros((Q_WIDTH,), F32), _pack_kv(kv_b)]).reshape(1, QGK_WIDTH)
    inv_freq = ROPE_THETA ** (-jnp.arange(0, ROT_DIM, 2, dtype=F32) / ROT_DIM)
    invf = jnp.tile(inv_freq, LANES // ROPE_SHIFT).reshape(1, LANES)
    pos_rep = jnp.repeat(positions.reshape(TOKENS // TOKENS_PER_ROW, TOKENS_PER_ROW), ROPE_SHIFT, axis=1)
    cos_t, sin_t = _rope_tables(pos_rep, invf, _rope_select())
    qgk = _qgk_proj(x1, cos_t, sin_t, w_qgk, bias, _rope_masks())
    o = _attention(qgk, b_sinks[0], _window_mask_bias())
    out = _out_ln(o, b_w_out[0].astype(BF16), x1, ln_g[1].reshape(1, D_MODEL),
                  ln_b[1].reshape(1, D_MODEL), 512, "swa_out_ln")
    return out.reshape(BATCH, SEQ, D_MODEL)
```

```python
import math

import numpy as np
import jax
import jax.numpy as jnp
from jax import lax
from jax.experimental import pallas as pl
from jax.experimental.pallas import tpu as pltpu

D_MODEL = 2048
BATCH = 8
SEQ = 4096
TOKENS = BATCH * SEQ
DEPTH = 2

D_INNER = 4096
SSM_HEAD_DIM = 64
SSM_HEADS = 64
SSM_GROUPS = 8
HEADS_PER_GROUP = SSM_HEADS // SSM_GROUPS
GROUP_WIDTH = D_INNER // SSM_GROUPS
D_STATE = 128
D_CONV = 4
CHUNK = 128
GN = SSM_GROUPS * D_STATE
CONV_DIM = D_INNER + 2 * GN
ZX_WIDTH = D_INNER + CONV_DIM

ATT_HEAD_DIM = 64
ATT_HEADS = 32
KV_HEADS = 4
Q_PER_KV = 8
Q_WIDTH = 2048
KV_WIDTH = 256
WINDOW = 128
ROT_DIM = 16
ROPE_SHIFT = ROT_DIM // 2
ROPE_THETA = 500000.0
KV_PACK = 4 * ATT_HEAD_DIM
QGK_WIDTH = 2 * Q_WIDTH + KV_HEADS * KV_PACK

ALPHA = (2.0 * DEPTH) ** 0.25
EPS = 1e-5
LOG2E = math.log2(math.e)

LANES = 128
SUBLANES = 8
BF16_SUBLANES = 16
VMEM_LIMIT_BYTES = 56 * 1024 * 1024

F32 = jnp.float32
BF16 = jnp.bfloat16


def _silu(v):
    h = 0.5 * v
    return h + h * jnp.tanh(h)


def _split_bf16(v, terms):
    parts = []
    rem = v
    for _ in range(terms):
        p = rem.astype(BF16)
        parts.append(p)
        rem = rem - p.astype(F32)
    return parts


PROJ_BM = 1024
PROJ_BN = 1024
PROJ_SUB = 256
PROJ_TILES = ZX_WIDTH // PROJ_BN
Z_TILES = D_INNER // PROJ_BN


def _in_proj_kernel(x_ref, w_ref, wdt_ref, zx_ref, dt_ref, xb_ref):
    j = pl.program_id(1)

    def gate_tile(first):
        for r in range(PROJ_BM // PROJ_SUB):
            rows = slice(r * PROJ_SUB, (r + 1) * PROJ_SUB)
            if first:
                xb_ref[rows, :] = x_ref[rows, :].astype(BF16)
                dt_ref[rows, :] = jnp.dot(xb_ref[rows, :], wdt_ref[...], preferred_element_type=F32)
            acc = jnp.dot(xb_ref[rows, :], w_ref[...], preferred_element_type=F32)
            zx_ref[rows, :] = _silu(acc).astype(BF16)

    @pl.when(j == 0)
    def _():
        gate_tile(True)

    @pl.when((j > 0) & (j < Z_TILES))
    def _():
        gate_tile(False)

    @pl.when(j >= Z_TILES)
    def _():
        zx_ref[...] = jnp.dot(xb_ref[...], w_ref[...], preferred_element_type=F32).astype(BF16)


def _in_proj(x2, w_in, w_dt):
    grid = (TOKENS // PROJ_BM, PROJ_TILES)
    return pl.pallas_call(
        _in_proj_kernel,
        grid=grid,
        in_specs=[
            pl.BlockSpec((PROJ_BM, D_MODEL), lambda i, j: (i, 0)),
            pl.BlockSpec((D_MODEL, PROJ_BN), lambda i, j: (0, j)),
            pl.BlockSpec((D_MODEL, LANES), lambda i, j: (0, 0)),
        ],
        out_specs=[
            pl.BlockSpec((PROJ_BM, PROJ_BN), lambda i, j: (i, j)),
            pl.BlockSpec((PROJ_BM, LANES), lambda i, j: (i, 0)),
        ],
        out_shape=[
            jax.ShapeDtypeStruct((TOKENS, ZX_WIDTH), BF16),
            jax.ShapeDtypeStruct((TOKENS, LANES), F32),
        ],
        scratch_shapes=[pltpu.VMEM((PROJ_BM, D_MODEL), BF16)],
        compiler_params=pltpu.CompilerParams(
            dimension_semantics=("arbitrary", "arbitrary"),
            vmem_limit_bytes=VMEM_LIMIT_BYTES),
        name="in_proj",
    )(x2, w_in, w_dt)


XS_HALF = D_INNER // 2
EXPAND_ROWS = 2 * CHUNK + BF16_SUBLANES
CONV_SLABS = CONV_DIM // LANES
CONV_STRIDE = 4
CHUNKS_PER_SEQ = SEQ // CHUNK
OUT_SLICES = SSM_GROUPS


def _ssd_kernel(z_ref, xa_ref, xb_ref, bc_ref, dt_ref, cw_ref, cb_ref, dtb_ref, alog_ref,
                dsk_ref, nw_ref, e2_ref, wout_ref, res_ref, lng_ref, lnb_ref, o_ref,
                state_ref, stage_ref, conv_ref, y_ref, yprev_ref, h_ref):
    step = pl.program_id(0)

    @pl.when(step % CHUNKS_PER_SEQ == 0)
    def _():
        state_ref[...] = jnp.zeros_like(state_ref)
        for s in range(CONV_SLABS):
            stage_ref[s, 0:SUBLANES, :] = jnp.zeros((SUBLANES, LANES), F32)

    @pl.when(step == 0)
    def _():
        y_ref[...] = jnp.zeros_like(y_ref)

    yprev_ref[...] = y_ref[...]

    per_residue = CHUNK // CONV_STRIDE
    slabs_per_src = XS_HALF // LANES

    def conv_slab(slab):
        src = (xa_ref, xb_ref, bc_ref)[slab // slabs_per_src]
        s = slab % slabs_per_src
        stage_ref[slab, SUBLANES:SUBLANES + CHUNK, :] = src[:, s * LANES:(s + 1) * LANES].astype(F32)
        cols = slice(slab * LANES, (slab + 1) * LANES)
        taps = [cw_ref[D_CONV - 1 - j:D_CONV - j, cols] for j in range(D_CONV)]
        bias = cb_ref[:, cols]
        for k in range(CONV_STRIDE):
            acc = bias
            for j in range(D_CONV):
                acc = acc + stage_ref[slab, pl.ds(SUBLANES + k - j, per_residue, stride=CONV_STRIDE), :] * taps[j]
            conv_ref[slab, pl.ds(k, per_residue, stride=CONV_STRIDE), :] = _silu(acc)
        stage_ref[slab, 0:SUBLANES, :] = stage_ref[slab, CHUNK:CHUNK + SUBLANES, :]

    v = dt_ref[...] + dtb_ref[...]
    dt = jnp.maximum(v, 0.0) + jnp.log1p(jnp.exp(-jnp.abs(v)))
    da = dt * (-jnp.exp(alog_ref[...]))
    rq = lax.broadcasted_iota(jnp.int32, (CHUNK, CHUNK), 0)
    cs = lax.broadcasted_iota(jnp.int32, (CHUNK, CHUNK), 1)
    tri = rq >= cs
    cum3 = jnp.dot(jnp.where(tri, 1.0, 0.0).astype(BF16),
                   jnp.concatenate(_split_bf16(da, 3), axis=1), preferred_element_type=F32)
    cum = cum3[:, 0:LANES] + cum3[:, LANES:2 * LANES] + cum3[:, 2 * LANES:3 * LANES]
    c2 = cum * LOG2E
    c2_src_t = (c2 - jnp.log2(dt)).T
    last = cum[CHUNK - 1:CHUNK, :]
    stack = jnp.concatenate(
        [jnp.exp(cum), dt * jnp.exp(last - cum),
         jnp.broadcast_to(jnp.exp(last), (BF16_SUBLANES, LANES))], axis=0)
    expand_lhs = jnp.concatenate(_split_bf16(stack, 2), axis=1)

    lane = lax.broadcasted_iota(jnp.int32, (CHUNK, LANES), 1)
    keep_lo = jnp.where(lane < SSM_HEAD_DIM, 1.0, 0.0).astype(BF16)
    keep_hi = jnp.where(lane >= SSM_HEAD_DIM, 1.0, 0.0).astype(BF16)

    out_cols = D_MODEL // OUT_SLICES
    for g in range(SSM_GROUPS):
        gs = slice(g * GROUP_WIDTH, (g + 1) * GROUP_WIDTH)
        os_ = slice(g * out_cols, (g + 1) * out_cols)
        h_ref[:, os_] = jnp.dot(yprev_ref[...], wout_ref[:, os_], preferred_element_type=F32)
        slabs_per_group = GROUP_WIDTH // LANES
        for slab in ([g * slabs_per_group + q for q in range(slabs_per_group)]
                     + [D_INNER // LANES + g, (D_INNER + GN) // LANES + g]):
            conv_slab(slab)

        ex = jnp.dot(expand_lhs, e2_ref[:, gs], preferred_element_type=F32)
        ecum_x = ex[0:CHUNK]
        dstate_x = ex[CHUNK:2 * CHUNK]
        elast_x = ex[2 * CHUNK:2 * CHUNK + 1]

        xs = jnp.concatenate([conv_ref[g * slabs_per_group + q] for q in range(slabs_per_group)], axis=1)
        xsb = xs.astype(BF16)
        bm = conv_ref[D_INNER // LANES + g]
        bb = bm.astype(BF16)
        cb = conv_ref[(D_INNER + GN) // LANES + g].astype(BF16)
        cbt = lax.dot_general(cb, bb, (((1,), (1,)), ((), ())),
                              preferred_element_type=F32).astype(BF16)

        s_prev = state_ref[:, gs]
        y_off = jnp.dot(cb, s_prev.astype(BF16), preferred_element_type=F32) * ecum_x
        state_ref[:, gs] = s_prev * elast_x + jnp.dot(
            bm.T.astype(BF16), (xs * dstate_x).astype(BF16), preferred_element_type=F32)

        pairs = []
        for p in range(HEADS_PER_GROUP // 2):
            xp = xsb[:, p * LANES:(p + 1) * LANES]
            rhs = jnp.concatenate([xp * keep_lo, xp * keep_hi], axis=0)
            ms = []
            for half in range(2):
                h = g * HEADS_PER_GROUP + 2 * p + half
                decay = jnp.where(tri, jnp.exp2(c2[:, h:h + 1] - c2_src_t[h:h + 1, :]), 0.0)
                ms.append(cbt * decay.astype(BF16))
            pairs.append(jnp.dot(jnp.concatenate(ms, axis=1), rhs, preferred_element_type=F32))
        y = jnp.concatenate(pairs, axis=1) + y_off + xs * dsk_ref[:, gs]
        y = y * z_ref[:, gs].astype(F32)
        ms = jnp.mean(y * y, axis=-1, keepdims=True)
        y_ref[:, gs] = (y * lax.rsqrt(ms + EPS) * nw_ref[:, gs]).astype(BF16)

    v = ALPHA * res_ref[...] + h_ref[...]
    mu = jnp.mean(v, axis=-1, keepdims=True)
    d = v - mu
    var = jnp.mean(d * d, axis=-1, keepdims=True)
    o_ref[...] = d * lax.rsqrt(var + EPS) * lng_ref[...] + lnb_ref[...]


def _ssd(zx, dt_raw, conv_w, conv_b, dt_bias, a_log, d_skip_x, norm_w, e2, w_out, x2, ln_g, ln_b):
    n_chunks = TOKENS // CHUNK
    cur = lambda s: jnp.minimum(s, n_chunks - 1)
    prev = lambda s: jnp.maximum(s - 1, 0)
    const = lambda s: (0, 0)
    zx_cols = ZX_WIDTH // XS_HALF
    assert zx_cols == 5
    return pl.pallas_call(
        _ssd_kernel,
        grid=(n_chunks + 1,),
        in_specs=[
            pl.BlockSpec((CHUNK, D_INNER), lambda s: (cur(s), 0)),
            pl.BlockSpec((CHUNK, XS_HALF), lambda s: (cur(s), 2)),
            pl.BlockSpec((CHUNK, XS_HALF), lambda s: (cur(s), 3)),
            pl.BlockSpec((CHUNK, XS_HALF), lambda s: (cur(s), 4)),
            pl.BlockSpec((CHUNK, LANES), lambda s: (cur(s), 0)),
            pl.BlockSpec((D_CONV, CONV_DIM), const),
            pl.BlockSpec((1, CONV_DIM), const),
            pl.BlockSpec((1, LANES), const),
            pl.BlockSpec((1, LANES), const),
            pl.BlockSpec((1, D_INNER), const),
            pl.BlockSpec((1, D_INNER), const),
            pl.BlockSpec((2 * LANES, D_INNER), const, pipeline_mode=pl.Buffered(1)),
            pl.BlockSpec((D_INNER, D_MODEL), const, pipeline_mode=pl.Buffered(1)),
            pl.BlockSpec((CHUNK, D_MODEL), lambda s: (prev(s), 0)),
            pl.BlockSpec((1, D_MODEL), const),
            pl.BlockSpec((1, D_MODEL), const),
        ],
        out_specs=pl.BlockSpec((CHUNK, D_MODEL), lambda s: (prev(s), 0)),
        out_shape=jax.ShapeDtypeStruct((TOKENS, D_MODEL), F32),
        scratch_shapes=[
            pltpu.VMEM((D_STATE, D_INNER), F32),
            pltpu.VMEM((CONV_SLABS, SUBLANES + CHUNK, LANES), F32),
            pltpu.VMEM((CONV_SLABS, CHUNK, LANES), F32),
            pltpu.VMEM((CHUNK, D_INNER), BF16),
            pltpu.VMEM((CHUNK, D_INNER), BF16),
            pltpu.VMEM((CHUNK, D_MODEL), F32),
        ],
        compiler_params=pltpu.CompilerParams(
            dimension_semantics=("arbitrary",),
            vmem_limit_bytes=VMEM_LIMIT_BYTES),
        name="ssd_out_ln",
    )(zx, zx, zx, zx, dt_raw, conv_w, conv_b, dt_bias, a_log, d_skip_x, norm_w, e2, w_out, x2, ln_g, ln_b)


TOKENS_PER_ROW = LANES // ROPE_SHIFT
COMPACT_ROWS = SEQ // TOKENS_PER_ROW
SPLIT_TERMS = 3


def _rope_table_kernel(pos_ref, invf_ref, sel_ref, cos_ref, sin_ref):
    ang = pos_ref[...].astype(F32) * invf_ref[...]
    for table, out_ref in ((jnp.cos(ang), cos_ref), (jnp.sin(ang), sin_ref)):
        lhs = jnp.concatenate(_split_bf16(table, SPLIT_TERMS), axis=1)
        for u in range(TOKENS_PER_ROW):
            out_ref[pl.ds(u, COMPACT_ROWS, stride=TOKENS_PER_ROW), :] = jnp.dot(
                lhs, sel_ref[u], preferred_element_type=F32)


def _rope_tables(pos_rep, invf, sel):
    return pl.pallas_call(
        _rope_table_kernel,
        grid=(BATCH,),
        in_specs=[
            pl.BlockSpec((COMPACT_ROWS, LANES), lambda b: (b, 0)),
            pl.BlockSpec((1, LANES), lambda b: (0, 0)),
            pl.BlockSpec((TOKENS_PER_ROW, SPLIT_TERMS * LANES, LANES), lambda b: (0, 0, 0)),
        ],
        out_specs=[pl.BlockSpec((SEQ, LANES), lambda b: (b, 0))] * 2,
        out_shape=[jax.ShapeDtypeStruct((TOKENS, LANES), F32)] * 2,
        compiler_params=pltpu.CompilerParams(
            dimension_semantics=("arbitrary",),
            vmem_limit_bytes=VMEM_LIMIT_BYTES),
        name="rope_tables",
    )(pos_rep, invf, sel)


QGK_BM = 1024
ROPE_PERIOD = KV_PACK
Q_TILES = Q_WIDTH // PROJ_BN


def _qgk_kernel(x_ref, cos_ref, sin_ref, w_ref, bias_ref, mask_ref, o_ref, xb_ref, tab_ref):
    j = pl.program_id(1)

    def rope_tile(first):
        t = 0 if first else jnp.where(j >= 2 * Q_TILES, 1, 0)
        reps = PROJ_BN // ROPE_PERIOD
        for r in range(QGK_BM // PROJ_SUB):
            rows = slice(r * PROJ_SUB, (r + 1) * PROJ_SUB)
            if first:
                xb_ref[rows, :] = x_ref[rows, :].astype(BF16)
                cos_p = jnp.tile(cos_ref[rows, :], (1, ROPE_PERIOD // LANES))
                sin_p = jnp.tile(sin_ref[rows, :], (1, ROPE_PERIOD // LANES))
                for kind in range(2):
                    m = mask_ref[kind]
                    tab_ref[kind, 0, rows, :] = 1.0 + m[0:1] * (cos_p - 1.0)
                    tab_ref[kind, 1, rows, :] = m[1:2] * sin_p
                    tab_ref[kind, 2, rows, :] = m[2:3] * sin_p
            acc = jnp.dot(xb_ref[rows, :], w_ref[...], preferred_element_type=F32) + bias_ref[...]
            upper = pltpu.roll(acc, PROJ_BN - ROPE_SHIFT, axis=1)
            lower = pltpu.roll(acc, ROPE_SHIFT, axis=1)
            out = (acc * jnp.tile(tab_ref[t, 0, rows, :], (1, reps))
                   + upper * jnp.tile(tab_ref[t, 1, rows, :], (1, reps))
                   + lower * jnp.tile(tab_ref[t, 2, rows, :], (1, reps)))
            o_ref[rows, :] = out.astype(BF16)

    has_rope = (j < Q_TILES) | (j >= 2 * Q_TILES)

    @pl.when(j == 0)
    def _():
        rope_tile(True)

    @pl.when(has_rope & (j > 0))
    def _():
        rope_tile(False)

    @pl.when(jnp.logical_not(has_rope))
    def _():
        for r in range(QGK_BM // PROJ_SUB):
            rows = slice(r * PROJ_SUB, (r + 1) * PROJ_SUB)
            acc = jnp.dot(xb_ref[rows, :], w_ref[...], preferred_element_type=F32) + bias_ref[...]
            o_ref[rows, :] = _silu(acc).astype(BF16)


def _qgk_proj(x1, cos_t, sin_t, w, bias, masks):
    grid = (TOKENS // QGK_BM, QGK_WIDTH // PROJ_BN)
    return pl.pallas_call(
        _qgk_kernel,
        grid=grid,
        in_specs=[
            pl.BlockSpec((QGK_BM, D_MODEL), lambda i, j: (i, 0)),
            pl.BlockSpec((QGK_BM, LANES), lambda i, j: (i, 0)),
            pl.BlockSpec((QGK_BM, LANES), lambda i, j: (i, 0)),
            pl.BlockSpec((D_MODEL, PROJ_BN), lambda i, j: (0, j)),
            pl.BlockSpec((1, PROJ_BN), lambda i, j: (0, j)),
            pl.BlockSpec((2, SUBLANES, ROPE_PERIOD), lambda i, j: (0, 0, 0)),
        ],
        out_specs=pl.BlockSpec((QGK_BM, PROJ_BN), lambda i, j: (i, j)),
        out_shape=jax.ShapeDtypeStruct((TOKENS, QGK_WIDTH), BF16),
        scratch_shapes=[
            pltpu.VMEM((QGK_BM, D_MODEL), BF16),
            pltpu.VMEM((2, 3, QGK_BM, ROPE_PERIOD), F32),
        ],
        compiler_params=pltpu.CompilerParams(
            dimension_semantics=("arbitrary", "arbitrary"),
            vmem_limit_bytes=VMEM_LIMIT_BYTES),
        name="qgk_proj",
    )(x1, cos_t, sin_t, w, bias, masks)


PAIRS_PER_KV = Q_PER_KV // 2
STACK_ROWS = PAIRS_PER_KV * WINDOW
KV_GROUP_Q = Q_PER_KV * ATT_HEAD_DIM
SCORE_SCALE_LOG2 = ATT_HEAD_DIM ** -0.5 * LOG2E


def _attn_kernel(sink_ref, mask_ref, q_ref, gate_ref, kvp_ref, kvc_ref, wout_ref, res_ref, lng_ref, lnb_ref,
                 out_ref, o_ref, oprev_ref, h_ref):
    @pl.when(pl.program_id(0) == 0)
    def _():
        o_ref[...] = jnp.zeros_like(o_ref)

    oprev_ref[...] = o_ref[...]
    out_cols = D_MODEL // (2 * KV_HEADS)
    mask_bias = jnp.tile(mask_ref[0], (PAIRS_PER_KV, 1))
    lane = lax.broadcasted_iota(jnp.int32, (STACK_ROWS, LANES), 1)
    low_half = lane < ATT_HEAD_DIM
    kv_lane = lax.broadcasted_iota(jnp.int32, (2 * WINDOW, LANES), 1)
    kv_low = kv_lane < ATT_HEAD_DIM
    pair_of_row = lax.broadcasted_iota(jnp.int32, (STACK_ROWS, 1), 0) // WINDOW

    values, probs, sink_terms = [], [], []
    for hk in range(KV_HEADS):
        ks = slice(hk * KV_PACK, (hk + 1) * KV_PACK)
        kv = jnp.concatenate([kvp_ref[:, ks], kvc_ref[:, ks]], axis=0)
        kv_kv = kv[:, 0:LANES]
        kv_vk = kv[:, LANES:2 * LANES]
        values.append((jnp.where(kv_low, kv_vk, jnp.ones_like(kv_vk)),
                       jnp.where(kv_low, jnp.ones_like(kv_kv), kv_kv)))
        qs = jnp.concatenate(
            [q_ref[:, hk * KV_GROUP_Q + p * LANES:hk * KV_GROUP_Q + (p + 1) * LANES]
             for p in range(PAIRS_PER_KV)], axis=0)
        for half in range(2):
            os_ = slice((2 * hk + half) * out_cols, (2 * hk + half + 1) * out_cols)
            h_ref[:, os_] = jnp.dot(oprev_ref[...], wout_ref[:, os_], preferred_element_type=F32)

            qm = jnp.where(low_half if half == 0 else ~low_half, qs, jnp.zeros_like(qs))
            keys = kv_kv if half == 0 else kv_vk
            s = lax.dot_general(qm, keys, (((1,), (1,)), ((), ())), preferred_element_type=F32)
            s = s * SCORE_SCALE_LOG2 + mask_bias
            sk = jnp.zeros((STACK_ROWS, 1), F32)
            for p in range(PAIRS_PER_KV):
                sk = jnp.where(pair_of_row == p, sink_ref[hk * Q_PER_KV + 2 * p + half] * LOG2E, sk)
            m = jnp.maximum(jnp.max(s, axis=-1, keepdims=True), sk)
            probs.append(jnp.exp2(s - m).astype(BF16))
            sink_terms.append(jnp.exp2(sk - m))

    accs = [jnp.dot(probs[2 * hk + half], values[hk][half], preferred_element_type=F32)
            for hk in range(KV_HEADS) for half in range(2)]

    for hk in range(KV_HEADS):
        acc_lo, acc_hi = accs[2 * hk], accs[2 * hk + 1]
        num = jnp.where(low_half, acc_lo, acc_hi)
        den = (pltpu.roll(jnp.where(low_half, acc_hi, acc_lo), ATT_HEAD_DIM, axis=1)
               + jnp.where(low_half, sink_terms[2 * hk], sink_terms[2 * hk + 1]))
        o_pairs = num / den
        for p in range(PAIRS_PER_KV):
            cols = slice(hk * KV_GROUP_Q + p * LANES, hk * KV_GROUP_Q + (p + 1) * LANES)
            o_ref[:, cols] = (o_pairs[p * WINDOW:(p + 1) * WINDOW]
                              * gate_ref[:, cols].astype(F32)).astype(BF16)

    v = ALPHA * res_ref[...] + h_ref[...]
    mu = jnp.mean(v, axis=-1, keepdims=True)
    d = v - mu
    var = jnp.mean(d * d, axis=-1, keepdims=True)
    out_ref[...] = d * lax.rsqrt(var + EPS) * lng_ref[...] + lnb_ref[...]


def _attention(qgk, sinks, mask_bias, w_out, x1, ln_g, ln_b):
    nb = SEQ // WINDOW
    n_blocks = TOKENS // WINDOW
    cur = lambda s: jnp.minimum(s, n_blocks - 1)
    prev = lambda s: jnp.maximum(s - 1, 0)
    not_first = lambda s: jnp.minimum(cur(s) % nb, 1)
    kv_col = (2 * Q_WIDTH) // (KV_HEADS * KV_PACK)
    const = lambda s: (0, 0)
    return pl.pallas_call(
        _attn_kernel,
        grid=(n_blocks + 1,),
        in_specs=[
            pl.BlockSpec(memory_space=pltpu.SMEM),
            pl.BlockSpec((1, WINDOW, 2 * WINDOW), lambda s: (not_first(s), 0, 0)),
            pl.BlockSpec((WINDOW, Q_WIDTH), lambda s: (cur(s), 0)),
            pl.BlockSpec((WINDOW, Q_WIDTH), lambda s: (cur(s), 1)),
            pl.BlockSpec((WINDOW, KV_HEADS * KV_PACK), lambda s: (cur(s) - not_first(s), kv_col)),
            pl.BlockSpec((WINDOW, KV_HEADS * KV_PACK), lambda s: (cur(s), kv_col)),
            pl.BlockSpec((Q_WIDTH, D_MODEL), const, pipeline_mode=pl.Buffered(1)),
            pl.BlockSpec((WINDOW, D_MODEL), lambda s: (prev(s), 0)),
            pl.BlockSpec((1, D_MODEL), const),
            pl.BlockSpec((1, D_MODEL), const),
        ],
        out_specs=pl.BlockSpec((WINDOW, D_MODEL), lambda s: (prev(s), 0)),
        out_shape=jax.ShapeDtypeStruct((TOKENS, D_MODEL), F32),
        scratch_shapes=[
            pltpu.VMEM((WINDOW, Q_WIDTH), BF16),
            pltpu.VMEM((WINDOW, Q_WIDTH), BF16),
            pltpu.VMEM((WINDOW, D_MODEL), F32),
        ],
        compiler_params=pltpu.CompilerParams(
            dimension_semantics=("arbitrary",),
            vmem_limit_bytes=VMEM_LIMIT_BYTES),
        name="swa_out_ln",
    )(sinks, mask_bias, qgk, qgk, qgk, qgk, w_out, x1, ln_g, ln_b)


def _expand_matrix():
    e = np.zeros((2 * LANES, D_INNER), np.float32)
    for h in range(SSM_HEADS):
        e[h, h * SSM_HEAD_DIM:(h + 1) * SSM_HEAD_DIM] = 1.0
        e[LANES + h, h * SSM_HEAD_DIM:(h + 1) * SSM_HEAD_DIM] = 1.0
    return jnp.asarray(e, BF16)


def _rope_masks():
    m = np.zeros((2, SUBLANES, ROPE_PERIOD), np.float32)
    d = np.arange(ATT_HEAD_DIM)
    sect = np.stack([1.0 * (d < ROT_DIM), -1.0 * (d < ROPE_SHIFT),
                     1.0 * ((d >= ROPE_SHIFT) & (d < ROT_DIM))]).astype(np.float32)
    for s in range(ROPE_PERIOD // ATT_HEAD_DIM):
        m[0, 0:3, s * ATT_HEAD_DIM:(s + 1) * ATT_HEAD_DIM] = sect
        if s in (0, 3):
            m[1, 0:3, s * ATT_HEAD_DIM:(s + 1) * ATT_HEAD_DIM] = sect
    return jnp.asarray(m)


def _rope_select():
    sel = np.zeros((TOKENS_PER_ROW, SPLIT_TERMS * LANES, LANES), np.float32)
    lane = np.arange(LANES)
    for u in range(TOKENS_PER_ROW):
        for t in range(SPLIT_TERMS):
            sel[u, t * LANES + u * ROPE_SHIFT + lane % ROPE_SHIFT, lane] = 1.0
    return jnp.asarray(sel, BF16)


def _window_mask_bias():
    qi = np.arange(WINDOW)[:, None]
    col = np.arange(2 * WINDOW)[None, :]
    band = (col > qi) & (col <= qi + WINDOW)
    first = band & (col >= WINDOW)
    return jnp.asarray(np.where(np.stack([first, band]), 0.0, -np.inf).astype(np.float32))


def _pack_kv(a):
    k = a[..., :KV_WIDTH].reshape(a.shape[:-1] + (KV_HEADS, ATT_HEAD_DIM))
    v = a[..., KV_WIDTH:].reshape(a.shape[:-1] + (KV_HEADS, ATT_HEAD_DIM))
    return jnp.concatenate([k, v, v, k], axis=-1).reshape(a.shape[:-1] + (KV_HEADS * KV_PACK,))


def kernel(x, positions, ln_g, ln_b, a_w_in, a_conv_w, a_conv_b, a_dt_bias, a_log, a_d, a_norm_w, a_w_out, kv_w, kv_b, b_w_in, b_q_bias, b_sinks, b_w_out):
    x2 = x.reshape(TOKENS, D_MODEL)

    w_in = a_w_in[0].astype(BF16)
    w_dt = jnp.pad(w_in[:, ZX_WIDTH:], ((0, 0), (0, LANES - SSM_HEADS)))
    zx, dt_raw = _in_proj(x2, w_in, w_dt)

    pad_heads = lambda a: jnp.pad(a, (0, LANES - SSM_HEADS)).reshape(1, LANES)
    x1 = _ssd(zx, dt_raw, a_conv_w[0], a_conv_b[0].reshape(1, CONV_DIM),
              pad_heads(a_dt_bias[0]), pad_heads(a_log[0]),
              jnp.repeat(a_d[0], SSM_HEAD_DIM).reshape(1, D_INNER),
              a_norm_w[0].reshape(1, D_INNER), _expand_matrix(),
              a_w_out[0].astype(BF16), x2, ln_g[0].reshape(1, D_MODEL), ln_b[0].reshape(1, D_MODEL))

    w_qgk = jnp.concatenate([b_w_in[0].astype(BF16), _pack_kv(kv_w.astype(BF16))], axis=1)
    bias = jnp.concatenate([b_q_bias[0], jnp.zeros((Q_WIDTH,), F32), _pack_kv(kv_b)]).reshape(1, QGK_WIDTH)
    inv_freq = ROPE_THETA ** (-jnp.arange(0, ROT_DIM, 2, dtype=F32) / ROT_DIM)
    invf = jnp.tile(inv_freq, LANES // ROPE_SHIFT).reshape(1, LANES)
    pos_rep = jnp.repeat(positions.reshape(TOKENS // TOKENS_PER_ROW, TOKENS_PER_ROW), ROPE_SHIFT, axis=1)
    cos_t, sin_t = _rope_tables(pos_rep, invf, _rope_select())
    qgk = _qgk_proj(x1, cos_t, sin_t, w_qgk, bias, _rope_masks())
    out = _attention(qgk, b_sinks[0], _window_mask_bias(), b_w_out[0].astype(BF16), x1,
                     ln_g[1].reshape(1, D_MODEL), ln_b[1].reshape(1, D_MODEL))
    return out.reshape(BATCH, SEQ, D_MODEL)
```

```python
import math

import numpy as np
import jax
import jax.numpy as jnp
from jax import lax
from jax.experimental import pallas as pl
from jax.experimental.pallas import tpu as pltpu

D_MODEL = 2048
BATCH = 8
SEQ = 4096
TOKENS = BATCH * SEQ
DEPTH = 2

D_INNER = 4096
SSM_HEAD_DIM = 64
SSM_HEADS = 64
SSM_GROUPS = 8
HEADS_PER_GROUP = SSM_HEADS // SSM_GROUPS
GROUP_WIDTH = D_INNER // SSM_GROUPS
D_STATE = 128
D_CONV = 4
CHUNK = 128
GN = SSM_GROUPS * D_STATE
CONV_DIM = D_INNER + 2 * GN
ZX_WIDTH = D_INNER + CONV_DIM

ATT_HEAD_DIM = 64
ATT_HEADS = 32
KV_HEADS = 4
Q_PER_KV = 8
Q_WIDTH = 2048
KV_WIDTH = 256
WINDOW = 128
ROT_DIM = 16
ROPE_SHIFT = ROT_DIM // 2
ROPE_THETA = 500000.0
KV_PACK = 4 * ATT_HEAD_DIM
QGK_WIDTH = 2 * Q_WIDTH + KV_HEADS * KV_PACK

ALPHA = (2.0 * DEPTH) ** 0.25
EPS = 1e-5
LOG2E = math.log2(math.e)

LANES = 128
SUBLANES = 8
BF16_SUBLANES = 16
VMEM_LIMIT_BYTES = 56 * 1024 * 1024

F32 = jnp.float32
BF16 = jnp.bfloat16


def _silu(v):
    h = 0.5 * v
    return h + h * jnp.tanh(h)


def _split_bf16(v, terms):
    parts = []
    rem = v
    for _ in range(terms):
        p = rem.astype(BF16)
        parts.append(p)
        rem = rem - p.astype(F32)
    return parts


PROJ_BM = 1024
PROJ_BN = 1024
PROJ_SUB = 256
PROJ_TILES = ZX_WIDTH // PROJ_BN
Z_TILES = D_INNER // PROJ_BN


def _in_proj_kernel(x_ref, w_ref, wdt_ref, zx_ref, dt_ref, xb_ref):
    j = pl.program_id(1)

    def gate_tile(first):
        for r in range(PROJ_BM // PROJ_SUB):
            rows = slice(r * PROJ_SUB, (r + 1) * PROJ_SUB)
            if first:
                xb_ref[rows, :] = x_ref[rows, :].astype(BF16)
                dt_ref[rows, :] = jnp.dot(xb_ref[rows, :], wdt_ref[...], preferred_element_type=F32)
            acc = jnp.dot(xb_ref[rows, :], w_ref[...], preferred_element_type=F32)
            zx_ref[rows, :] = _silu(acc).astype(BF16)

    @pl.when(j == 0)
    def _():
        gate_tile(True)

    @pl.when((j > 0) & (j < Z_TILES))
    def _():
        gate_tile(False)

    @pl.when(j >= Z_TILES)
    def _():
        zx_ref[...] = jnp.dot(xb_ref[...], w_ref[...], preferred_element_type=F32).astype(BF16)


def _in_proj(x2, w_in, w_dt):
    grid = (TOKENS // PROJ_BM, PROJ_TILES)
    return pl.pallas_call(
        _in_proj_kernel,
        grid=grid,
        in_specs=[
            pl.BlockSpec((PROJ_BM, D_MODEL), lambda i, j: (i, 0)),
            pl.BlockSpec((D_MODEL, PROJ_BN), lambda i, j: (0, j)),
            pl.BlockSpec((D_MODEL, LANES), lambda i, j: (0, 0)),
        ],
        out_specs=[
            pl.BlockSpec((PROJ_BM, PROJ_BN), lambda i, j: (i, j)),
            pl.BlockSpec((PROJ_BM, LANES), lambda i, j: (i, 0)),
        ],
        out_shape=[
            jax.ShapeDtypeStruct((TOKENS, ZX_WIDTH), BF16),
            jax.ShapeDtypeStruct((TOKENS, LANES), F32),
        ],
        scratch_shapes=[pltpu.VMEM((PROJ_BM, D_MODEL), BF16)],
        compiler_params=pltpu.CompilerParams(
            dimension_semantics=("arbitrary", "arbitrary"),
            vmem_limit_bytes=VMEM_LIMIT_BYTES),
        name="in_proj",
    )(x2, w_in, w_dt)


XS_HALF = D_INNER // 2
EXPAND_ROWS = 2 * CHUNK + BF16_SUBLANES
CONV_SLABS = CONV_DIM // LANES
CONV_STRIDE = 4
SUB_CHUNKS = 2
SSD_ROWS = SUB_CHUNKS * CHUNK
OUT_SLICES = SSM_GROUPS


def _mix_chunk(z_ref, xa_ref, xb_ref, bc_ref, dt_ref, cw_ref, cb_ref, dtb_ref, alog_ref,
               dsk_ref, nw_ref, e2_ref, wout_ref, state_ref, stage_ref, conv_ref, y_ref, yprev_ref, h_ref, sub):
    per_residue = CHUNK // CONV_STRIDE
    slabs_per_src = XS_HALF // LANES

    def conv_slab(slab):
        src = (xa_ref, xb_ref, bc_ref)[slab // slabs_per_src]
        s = slab % slabs_per_src
        stage_ref[slab, SUBLANES:SUBLANES + CHUNK, :] = src[:, s * LANES:(s + 1) * LANES].astype(F32)
        cols = slice(slab * LANES, (slab + 1) * LANES)
        taps = [cw_ref[D_CONV - 1 - j:D_CONV - j, cols] for j in range(D_CONV)]
        bias = cb_ref[:, cols]
        for k in range(CONV_STRIDE):
            acc = bias
            for j in range(D_CONV):
                acc = acc + stage_ref[slab, pl.ds(SUBLANES + k - j, per_residue, stride=CONV_STRIDE), :] * taps[j]
            conv_ref[slab, pl.ds(k, per_residue, stride=CONV_STRIDE), :] = _silu(acc)
        stage_ref[slab, 0:SUBLANES, :] = stage_ref[slab, CHUNK:CHUNK + SUBLANES, :]

    v = dt_ref[...] + dtb_ref[...]
    dt = jnp.maximum(v, 0.0) + jnp.log1p(jnp.exp(-jnp.abs(v)))
    da = dt * (-jnp.exp(alog_ref[...]))
    rq = lax.broadcasted_iota(jnp.int32, (CHUNK, CHUNK), 0)
    cs = lax.broadcasted_iota(jnp.int32, (CHUNK, CHUNK), 1)
    tri = rq >= cs
    cum3 = jnp.dot(jnp.where(tri, 1.0, 0.0).astype(BF16),
                   jnp.concatenate(_split_bf16(da, 3), axis=1), preferred_element_type=F32)
    cum = cum3[:, 0:LANES] + cum3[:, LANES:2 * LANES] + cum3[:, 2 * LANES:3 * LANES]
    c2 = cum * LOG2E
    c2_src_t = (c2 - jnp.log2(dt)).T
    last = cum[CHUNK - 1:CHUNK, :]
    stack = jnp.concatenate(
        [jnp.exp(cum), dt * jnp.exp(last - cum),
         jnp.broadcast_to(jnp.exp(last), (BF16_SUBLANES, LANES))], axis=0)
    expand_lhs = jnp.concatenate(_split_bf16(stack, 2), axis=1)

    lane = lax.broadcasted_iota(jnp.int32, (CHUNK, LANES), 1)
    keep_lo = jnp.where(lane < SSM_HEAD_DIM, 1.0, 0.0).astype(BF16)
    keep_hi = jnp.where(lane >= SSM_HEAD_DIM, 1.0, 0.0).astype(BF16)

    out_cols = D_MODEL // OUT_SLICES
    for g in range(SSM_GROUPS):
        gs = slice(g * GROUP_WIDTH, (g + 1) * GROUP_WIDTH)
        if g % SUB_CHUNKS == 0:
            n = sub * (SSM_GROUPS // SUB_CHUNKS) + g // SUB_CHUNKS
            os_ = slice(n * out_cols, (n + 1) * out_cols)
            h_ref[:, os_] = jnp.dot(yprev_ref[...], wout_ref[:, os_], preferred_element_type=F32)
        slabs_per_group = GROUP_WIDTH // LANES
        for slab in ([g * slabs_per_group + q for q in range(slabs_per_group)]
                     + [D_INNER // LANES + g, (D_INNER + GN) // LANES + g]):
            conv_slab(slab)

        ex = jnp.dot(expand_lhs, e2_ref[:, gs], preferred_element_type=F32)
        ecum_x = ex[0:CHUNK]
        dstate_x = ex[CHUNK:2 * CHUNK]
        elast_x = ex[2 * CHUNK:2 * CHUNK + 1]

        xs = jnp.concatenate([conv_ref[g * slabs_per_group + q] for q in range(slabs_per_group)], axis=1)
        xsb = xs.astype(BF16)
        bm = conv_ref[D_INNER // LANES + g]
        bb = bm.astype(BF16)
        cb = conv_ref[(D_INNER + GN) // LANES + g].astype(BF16)
        cbt = lax.dot_general(cb, bb, (((1,), (1,)), ((), ())),
                              preferred_element_type=F32).astype(BF16)

        s_prev = state_ref[:, gs]
        y_off = jnp.dot(cb, s_prev.astype(BF16), preferred_element_type=F32) * ecum_x
        state_ref[:, gs] = s_prev * elast_x + jnp.dot(
            bm.T.astype(BF16), (xs * dstate_x).astype(BF16), preferred_element_type=F32)

        pairs = []
        for p in range(HEADS_PER_GROUP // 2):
            xp = xsb[:, p * LANES:(p + 1) * LANES]
            rhs = jnp.concatenate([xp * keep_lo, xp * keep_hi], axis=0)
            ms = []
            for half in range(2):
                h = g * HEADS_PER_GROUP + 2 * p + half
                decay = jnp.where(tri, jnp.exp2(c2[:, h:h + 1] - c2_src_t[h:h + 1, :]), 0.0)
                ms.append(cbt * decay.astype(BF16))
            pairs.append(jnp.dot(jnp.concatenate(ms, axis=1), rhs, preferred_element_type=F32))
        y = jnp.concatenate(pairs, axis=1) + y_off + xs * dsk_ref[:, gs]
        y = y * z_ref[:, gs].astype(F32)
        ms = jnp.mean(y * y, axis=-1, keepdims=True)
        y_ref[:, gs] = (y * lax.rsqrt(ms + EPS) * nw_ref[:, gs]).astype(BF16)

def _ssd_kernel(z_ref, xa_ref, xb_ref, bc_ref, dt_ref, cw_ref, cb_ref, dtb_ref, alog_ref,
                dsk_ref, nw_ref, e2_ref, wout_ref, res_ref, lng_ref, lnb_ref, o_ref,
                state_ref, stage_ref, conv_ref, y_ref, yprev_ref, h_ref):
    step = pl.program_id(0)

    @pl.when(step % (SEQ // SSD_ROWS) == 0)
    def _():
        state_ref[...] = jnp.zeros_like(state_ref)
        for s in range(CONV_SLABS):
            stage_ref[s, 0:SUBLANES, :] = jnp.zeros((SUBLANES, LANES), F32)

    @pl.when(step == 0)
    def _():
        y_ref[...] = jnp.zeros_like(y_ref)

    yprev_ref[...] = y_ref[...]
    for sub in range(SUB_CHUNKS):
        rows = pl.ds(sub * CHUNK, CHUNK)
        _mix_chunk(z_ref.at[rows], xa_ref.at[rows], xb_ref.at[rows], bc_ref.at[rows], dt_ref.at[rows],
                   cw_ref, cb_ref, dtb_ref, alog_ref, dsk_ref, nw_ref, e2_ref, wout_ref,
                   state_ref, stage_ref, conv_ref, y_ref.at[rows], yprev_ref, h_ref, sub)

    v = ALPHA * res_ref[...] + h_ref[...]
    mu = jnp.mean(v, axis=-1, keepdims=True)
    d = v - mu
    var = jnp.mean(d * d, axis=-1, keepdims=True)
    o_ref[...] = d * lax.rsqrt(var + EPS) * lng_ref[...] + lnb_ref[...]


def _ssd(zx, dt_raw, conv_w, conv_b, dt_bias, a_log, d_skip_x, norm_w, e2, w_out, x2, ln_g, ln_b):
    n_blocks = TOKENS // SSD_ROWS
    cur = lambda s: jnp.minimum(s, n_blocks - 1)
    prev = lambda s: jnp.maximum(s - 1, 0)
    const = lambda s: (0, 0)
    zx_cols = ZX_WIDTH // XS_HALF
    assert zx_cols == 5
    return pl.pallas_call(
        _ssd_kernel,
        grid=(n_blocks + 1,),
        in_specs=[
            pl.BlockSpec((SSD_ROWS, D_INNER), lambda s: (cur(s), 0)),
            pl.BlockSpec((SSD_ROWS, XS_HALF), lambda s: (cur(s), 2)),
            pl.BlockSpec((SSD_ROWS, XS_HALF), lambda s: (cur(s), 3)),
            pl.BlockSpec((SSD_ROWS, XS_HALF), lambda s: (cur(s), 4)),
            pl.BlockSpec((SSD_ROWS, LANES), lambda s: (cur(s), 0)),
            pl.BlockSpec((D_CONV, CONV_DIM), const),
            pl.BlockSpec((1, CONV_DIM), const),
            pl.BlockSpec((1, LANES), const),
            pl.BlockSpec((1, LANES), const),
            pl.BlockSpec((1, D_INNER), const),
            pl.BlockSpec((1, D_INNER), const),
            pl.BlockSpec((2 * LANES, D_INNER), const, pipeline_mode=pl.Buffered(1)),
            pl.BlockSpec((D_INNER, D_MODEL), const, pipeline_mode=pl.Buffered(1)),
            pl.BlockSpec((SSD_ROWS, D_MODEL), lambda s: (prev(s), 0)),
            pl.BlockSpec((1, D_MODEL), const),
            pl.BlockSpec((1, D_MODEL), const),
        ],
        out_specs=pl.BlockSpec((SSD_ROWS, D_MODEL), lambda s: (prev(s), 0)),
        out_shape=jax.ShapeDtypeStruct((TOKENS, D_MODEL), F32),
        scratch_shapes=[
            pltpu.VMEM((D_STATE, D_INNER), F32),
            pltpu.VMEM((CONV_SLABS, SUBLANES + CHUNK, LANES), F32),
            pltpu.VMEM((CONV_SLABS, CHUNK, LANES), F32),
            pltpu.VMEM((SSD_ROWS, D_INNER), BF16),
            pltpu.VMEM((SSD_ROWS, D_INNER), BF16),
            pltpu.VMEM((SSD_ROWS, D_MODEL), F32),
        ],
        compiler_params=pltpu.CompilerParams(
            dimension_semantics=("arbitrary",),
            vmem_limit_bytes=VMEM_LIMIT_BYTES),
        name="ssd_out_ln",
    )(zx, zx, zx, zx, dt_raw, conv_w, conv_b, dt_bias, a_log, d_skip_x, norm_w, e2, w_out, x2, ln_g, ln_b)


TOKENS_PER_ROW = LANES // ROPE_SHIFT
COMPACT_ROWS = SEQ // TOKENS_PER_ROW
SPLIT_TERMS = 3


def _rope_table_kernel(pos_ref, invf_ref, sel_ref, cos_ref, sin_ref):
    ang = pos_ref[...].astype(F32) * invf_ref[...]
    for table, out_ref in ((jnp.cos(ang), cos_ref), (jnp.sin(ang), sin_ref)):
        lhs = jnp.concatenate(_split_bf16(table, SPLIT_TERMS), axis=1)
        for u in range(TOKENS_PER_ROW):
            out_ref[pl.ds(u, COMPACT_ROWS, stride=TOKENS_PER_ROW), :] = jnp.dot(
                lhs, sel_ref[u], preferred_element_type=F32)


def _rope_tables(pos_rep, invf, sel):
    return pl.pallas_call(
        _rope_table_kernel,
        grid=(BATCH,),
        in_specs=[
            pl.BlockSpec((COMPACT_ROWS, LANES), lambda b: (b, 0)),
            pl.BlockSpec((1, LANES), lambda b: (0, 0)),
            pl.BlockSpec((TOKENS_PER_ROW, SPLIT_TERMS * LANES, LANES), lambda b: (0, 0, 0)),
        ],
        out_specs=[pl.BlockSpec((SEQ, LANES), lambda b: (b, 0))] * 2,
        out_shape=[jax.ShapeDtypeStruct((TOKENS, LANES), F32)] * 2,
        compiler_params=pltpu.CompilerParams(
            dimension_semantics=("arbitrary",),
            vmem_limit_bytes=VMEM_LIMIT_BYTES),
        name="rope_tables",
    )(pos_rep, invf, sel)


QGK_BM = 1024
ROPE_PERIOD = KV_PACK
Q_TILES = Q_WIDTH // PROJ_BN


def _qgk_kernel(x_ref, cos_ref, sin_ref, w_ref, bias_ref, mask_ref, o_ref, xb_ref, tab_ref):
    j = pl.program_id(1)

    def rope_tile(first):
        t = 0 if first else jnp.where(j >= 2 * Q_TILES, 1, 0)
        reps = PROJ_BN // ROPE_PERIOD
        for r in range(QGK_BM // PROJ_SUB):
            rows = slice(r * PROJ_SUB, (r + 1) * PROJ_SUB)
            if first:
                xb_ref[rows, :] = x_ref[rows, :].astype(BF16)
                cos_p = jnp.tile(cos_ref[rows, :], (1, ROPE_PERIOD // LANES))
                sin_p = jnp.tile(sin_ref[rows, :], (1, ROPE_PERIOD // LANES))
                for kind in range(2):
                    m = mask_ref[kind]
                    tab_ref[kind, 0, rows, :] = 1.0 + m[0:1] * (cos_p - 1.0)
                    tab_ref[kind, 1, rows, :] = m[1:2] * sin_p
                    tab_ref[kind, 2, rows, :] = m[2:3] * sin_p
            acc = jnp.dot(xb_ref[rows, :], w_ref[...], preferred_element_type=F32) + bias_ref[...]
            upper = pltpu.roll(acc, PROJ_BN - ROPE_SHIFT, axis=1)
            lower = pltpu.roll(acc, ROPE_SHIFT, axis=1)
            out = (acc * jnp.tile(tab_ref[t, 0, rows, :], (1, reps))
                   + upper * jnp.tile(tab_ref[t, 1, rows, :], (1, reps))
                   + lower * jnp.tile(tab_ref[t, 2, rows, :], (1, reps)))
            o_ref[rows, :] = out.astype(BF16)

    has_rope = (j < Q_TILES) | (j >= 2 * Q_TILES)

    @pl.when(j == 0)
    def _():
        rope_tile(True)

    @pl.when(has_rope & (j > 0))
    def _():
        rope_tile(False)

    @pl.when(jnp.logical_not(has_rope))
    def _():
        for r in range(QGK_BM // PROJ_SUB):
            rows = slice(r * PROJ_SUB, (r + 1) * PROJ_SUB)
            acc = jnp.dot(xb_ref[rows, :], w_ref[...], preferred_element_type=F32) + bias_ref[...]
            o_ref[rows, :] = _silu(acc).astype(BF16)


def _qgk_proj(x1, cos_t, sin_t, w, bias, masks):
    grid = (TOKENS // QGK_BM, QGK_WIDTH // PROJ_BN)
    return pl.pallas_call(
        _qgk_kernel,
        grid=grid,
        in_specs=[
            pl.BlockSpec((QGK_BM, D_MODEL), lambda i, j: (i, 0)),
            pl.BlockSpec((QGK_BM, LANES), lambda i, j: (i, 0)),
            pl.BlockSpec((QGK_BM, LANES), lambda i, j: (i, 0)),
            pl.BlockSpec((D_MODEL, PROJ_BN), lambda i, j: (0, j)),
            pl.BlockSpec((1, PROJ_BN), lambda i, j: (0, j)),
            pl.BlockSpec((2, SUBLANES, ROPE_PERIOD), lambda i, j: (0, 0, 0)),
        ],
        out_specs=pl.BlockSpec((QGK_BM, PROJ_BN), lambda i, j: (i, j)),
        out_shape=jax.ShapeDtypeStruct((TOKENS, QGK_WIDTH), BF16),
        scratch_shapes=[
            pltpu.VMEM((QGK_BM, D_MODEL), BF16),
            pltpu.VMEM((2, 3, QGK_BM, ROPE_PERIOD), F32),
        ],
        compiler_params=pltpu.CompilerParams(
            dimension_semantics=("arbitrary", "arbitrary"),
            vmem_limit_bytes=VMEM_LIMIT_BYTES),
        name="qgk_proj",
    )(x1, cos_t, sin_t, w, bias, masks)


PAIRS_PER_KV = Q_PER_KV // 2
STACK_ROWS = PAIRS_PER_KV * WINDOW
KV_GROUP_Q = Q_PER_KV * ATT_HEAD_DIM
SCORE_SCALE_LOG2 = ATT_HEAD_DIM ** -0.5 * LOG2E


def _attn_kernel(sink_ref, mask_ref, q_ref, gate_ref, kvp_ref, kvc_ref, wout_ref, res_ref, lng_ref, lnb_ref,
                 out_ref, o_ref, oprev_ref, h_ref):
    @pl.when(pl.program_id(0) == 0)
    def _():
        o_ref[...] = jnp.zeros_like(o_ref)

    oprev_ref[...] = o_ref[...]
    out_cols = D_MODEL // (2 * KV_HEADS)
    mask_bias = jnp.tile(mask_ref[0], (PAIRS_PER_KV, 1))
    lane = lax.broadcasted_iota(jnp.int32, (STACK_ROWS, LANES), 1)
    low_half = lane < ATT_HEAD_DIM
    kv_lane = lax.broadcasted_iota(jnp.int32, (2 * WINDOW, LANES), 1)
    kv_low = kv_lane < ATT_HEAD_DIM
    pair_of_row = lax.broadcasted_iota(jnp.int32, (STACK_ROWS, 1), 0) // WINDOW

    values, probs, sink_terms = [], [], []
    for hk in range(KV_HEADS):
        ks = slice(hk * KV_PACK, (hk + 1) * KV_PACK)
        kv = jnp.concatenate([kvp_ref[:, ks], kvc_ref[:, ks]], axis=0)
        kv_kv = kv[:, 0:LANES]
        kv_vk = kv[:, LANES:2 * LANES]
        values.append((jnp.where(kv_low, kv_vk, jnp.ones_like(kv_vk)),
                       jnp.where(kv_low, jnp.ones_like(kv_kv), kv_kv)))
        qs = jnp.concatenate(
            [q_ref[:, hk * KV_GROUP_Q + p * LANES:hk * KV_GROUP_Q + (p + 1) * LANES]
             for p in range(PAIRS_PER_KV)], axis=0)
        for half in range(2):
            os_ = slice((2 * hk + half) * out_cols, (2 * hk + half + 1) * out_cols)
            h_ref[:, os_] = jnp.dot(oprev_ref[...], wout_ref[:, os_], preferred_element_type=F32)

            qm = jnp.where(low_half if half == 0 else ~low_half, qs, jnp.zeros_like(qs))
            keys = kv_kv if half == 0 else kv_vk
            s = lax.dot_general(qm, keys, (((1,), (1,)), ((), ())), preferred_element_type=F32)
            s = s * SCORE_SCALE_LOG2 + mask_bias
            sk = jnp.zeros((STACK_ROWS, 1), F32)
            for p in range(PAIRS_PER_KV):
                sk = jnp.where(pair_of_row == p, sink_ref[hk * Q_PER_KV + 2 * p + half] * LOG2E, sk)
            m = jnp.maximum(jnp.max(s, axis=-1, keepdims=True), sk)
            probs.append(jnp.exp2(s - m).astype(BF16))
            sink_terms.append(jnp.exp2(sk - m))

    accs = [jnp.dot(probs[2 * hk + half], values[hk][half], preferred_element_type=F32)
            for hk in range(KV_HEADS) for half in range(2)]

    for hk in range(KV_HEADS):
        acc_lo, acc_hi = accs[2 * hk], accs[2 * hk + 1]
        num = jnp.where(low_half, acc_lo, acc_hi)
        den = (pltpu.roll(jnp.where(low_half, acc_hi, acc_lo), ATT_HEAD_DIM, axis=1)
               + jnp.where(low_half, sink_terms[2 * hk], sink_terms[2 * hk + 1]))
        o_pairs = num / den
        for p in range(PAIRS_PER_KV):
            cols = slice(hk * KV_GROUP_Q + p * LANES, hk * KV_GROUP_Q + (p + 1) * LANES)
            o_ref[:, cols] = (o_pairs[p * WINDOW:(p + 1) * WINDOW]
                              * gate_ref[:, cols].astype(F32)).astype(BF16)

    v = ALPHA * res_ref[...] + h_ref[...]
    mu = jnp.mean(v, axis=-1, keepdims=True)
    d = v - mu
    var = jnp.mean(d * d, axis=-1, keepdims=True)
    out_ref[...] = d * lax.rsqrt(var + EPS) * lng_ref[...] + lnb_ref[...]


def _attention(qgk, sinks, mask_bias, w_out, x1, ln_g, ln_b):
    nb = SEQ // WINDOW
    n_blocks = TOKENS // WINDOW
    cur = lambda s: jnp.minimum(s, n_blocks - 1)
    prev = lambda s: jnp.maximum(s - 1, 0)
    not_first = lambda s: jnp.minimum(cur(s) % nb, 1)
    kv_col = (2 * Q_WIDTH) // (KV_HEADS * KV_PACK)
    const = lambda s: (0, 0)
    return pl.pallas_call(
        _attn_kernel,
        grid=(n_blocks + 1,),
        in_specs=[
            pl.BlockSpec(memory_space=pltpu.SMEM),
            pl.BlockSpec((1, WINDOW, 2 * WINDOW), lambda s: (not_first(s), 0, 0)),
            pl.BlockSpec((WINDOW, Q_WIDTH), lambda s: (cur(s), 0)),
            pl.BlockSpec((WINDOW, Q_WIDTH), lambda s: (cur(s), 1)),
            pl.BlockSpec((WINDOW, KV_HEADS * KV_PACK), lambda s: (cur(s) - not_first(s), kv_col)),
            pl.BlockSpec((WINDOW, KV_HEADS * KV_PACK), lambda s: (cur(s), kv_col)),
            pl.BlockSpec((Q_WIDTH, D_MODEL), const, pipeline_mode=pl.Buffered(1)),
            pl.BlockSpec((WINDOW, D_MODEL), lambda s: (prev(s), 0)),
            pl.BlockSpec((1, D_MODEL), const),
            pl.BlockSpec((1, D_MODEL), const),
        ],
        out_specs=pl.BlockSpec((WINDOW, D_MODEL), lambda s: (prev(s), 0)),
        out_shape=jax.ShapeDtypeStruct((TOKENS, D_MODEL), F32),
        scratch_shapes=[
            pltpu.VMEM((WINDOW, Q_WIDTH), BF16),
            pltpu.VMEM((WINDOW, Q_WIDTH), BF16),
            pltpu.VMEM((WINDOW, D_MODEL), F32),
        ],
        compiler_params=pltpu.CompilerParams(
            dimension_semantics=("arbitrary",),
            vmem_limit_bytes=VMEM_LIMIT_BYTES),
        name="swa_out_ln",
    )(sinks, mask_bias, qgk, qgk, qgk, qgk, w_out, x1, ln_g, ln_b)


def _expand_matrix():
    e = np.zeros((2 * LANES, D_INNER), np.float32)
    for h in range(SSM_HEADS):
        e[h, h * SSM_HEAD_DIM:(h + 1) * SSM_HEAD_DIM] = 1.0
        e[LANES + h, h * SSM_HEAD_DIM:(h + 1) * SSM_HEAD_DIM] = 1.0
    return jnp.asarray(e, BF16)


def _rope_masks():
    m = np.zeros((2, SUBLANES, ROPE_PERIOD), np.float32)
    d = np.arange(ATT_HEAD_DIM)
    sect = np.stack([1.0 * (d < ROT_DIM), -1.0 * (d < ROPE_SHIFT),
                     1.0 * ((d >= ROPE_SHIFT) & (d < ROT_DIM))]).astype(np.float32)
    for s in range(ROPE_PERIOD // ATT_HEAD_DIM):
        m[0, 0:3, s * ATT_HEAD_DIM:(s + 1) * ATT_HEAD_DIM] = sect
        if s in (0, 3):
            m[1, 0:3, s * ATT_HEAD_DIM:(s + 1) * ATT_HEAD_DIM] = sect
    return jnp.asarray(m)


def _rope_select():
    sel = np.zeros((TOKENS_PER_ROW, SPLIT_TERMS * LANES, LANES), np.float32)
    lane = np.arange(LANES)
    for u in range(TOKENS_PER_ROW):
        for t in range(SPLIT_TERMS):
            sel[u, t * LANES + u * ROPE_SHIFT + lane % ROPE_SHIFT, lane] = 1.0
    return jnp.asarray(sel, BF16)


def _window_mask_bias():
    qi = np.arange(WINDOW)[:, None]
    col = np.arange(2 * WINDOW)[None, :]
    band = (col > qi) & (col <= qi + WINDOW)
    first = band & (col >= WINDOW)
    return jnp.asarray(np.where(np.stack([first, band]), 0.0, -np.inf).astype(np.float32))


def _pack_kv(a):
    k = a[..., :KV_WIDTH].reshape(a.shape[:-1] + (KV_HEADS, ATT_HEAD_DIM))
    v = a[..., KV_WIDTH:].reshape(a.shape[:-1] + (KV_HEADS, ATT_HEAD_DIM))
    return jnp.concatenate([k, v, v, k], axis=-1).reshape(a.shape[:-1] + (KV_HEADS * KV_PACK,))


def kernel(x, positions, ln_g, ln_b, a_w_in, a_conv_w, a_conv_b, a_dt_bias, a_log, a_d, a_norm_w, a_w_out, kv_w, kv_b, b_w_in, b_q_bias, b_sinks, b_w_out):
    x2 = x.reshape(TOKENS, D_MODEL)

    w_in = a_w_in[0].astype(BF16)
    w_dt = jnp.pad(w_in[:, ZX_WIDTH:], ((0, 0), (0, LANES - SSM_HEADS)))
    zx, dt_raw = _in_proj(x2, w_in, w_dt)

    pad_heads = lambda a: jnp.pad(a, (0, LANES - SSM_HEADS)).reshape(1, LANES)
    x1 = _ssd(zx, dt_raw, a_conv_w[0], a_conv_b[0].reshape(1, CONV_DIM),
              pad_heads(a_dt_bias[0]), pad_heads(a_log[0]),
              jnp.repeat(a_d[0], SSM_HEAD_DIM).reshape(1, D_INNER),
              a_norm_w[0].reshape(1, D_INNER), _expand_matrix(),
              a_w_out[0].astype(BF16), x2, ln_g[0].reshape(1, D_MODEL), ln_b[0].reshape(1, D_MODEL))

    w_qgk = jnp.concatenate([b_w_in[0].astype(BF16), _pack_kv(kv_w.astype(BF16))], axis=1)
    bias = jnp.concatenate([b_q_bias[0], jnp.zeros((Q_WIDTH,), F32), _pack_kv(kv_b)]).reshape(1, QGK_WIDTH)
    inv_freq = ROPE_THETA ** (-jnp.arange(0, ROT_DIM, 2, dtype=F32) / ROT_DIM)
    invf = jnp.tile(inv_freq, LANES // ROPE_SHIFT).reshape(1, LANES)
    pos_rep = jnp.repeat(positions.reshape(TOKENS // TOKENS_PER_ROW, TOKENS_PER_ROW), ROPE_SHIFT, axis=1)
    cos_t, sin_t = _rope_tables(pos_rep, invf, _rope_select())
    qgk = _qgk_proj(x1, cos_t, sin_t, w_qgk, bias, _rope_masks())
    out = _attention(qgk, b_sinks[0], _window_mask_bias(), b_w_out[0].astype(BF16), x1,
                     ln_g[1].reshape(1, D_MODEL), ln_b[1].reshape(1, D_MODEL))
    return out.reshape(BATCH, SEQ, D_MODEL)
```
